```python
import math
import jax, jax.numpy as jnp
from jax import lax
import numpy as np

D_MODEL = 2048
BATCH = 8
SEQ = 2048
DEPTH = 2
DEC_BATCH = 16
DEC_SEQ = 16
PAST_LEN = 4096

CHUNK = 64
D_BRANCH = 1024
N_BRANCH = 3
POOL_WINDOWS = (2, 4, 8, 16)
POOL_GROUPS = len(POOL_WINDOWS)
POOL_GROUP_DIM = D_BRANCH // POOL_GROUPS
POOL_STATE = max(POOL_WINDOWS) - 1
CONV_WIDTH = 3
GM_CHUNK = 128
GM_HEADS = 8
GM_HEAD_DIM = D_BRANCH // GM_HEADS
N_EXPERTS = 32
TOP_K = 4
D_FF = D_MODEL
SWIGLU_LIMIT = 7.0
SWIGLU_ALPHA = 1.702
MOE_BLOCK = 128
DN_ALPHA = (2 * DEPTH) ** 0.25
DN_BETA = (8 * DEPTH) ** -0.25
LN_EPS = 1e-5
SPLITS = (N_BRANCH * D_MODEL, D_BRANCH, D_BRANCH, D_BRANCH, D_BRANCH, D_BRANCH, D_BRANCH)
SPLIT_POINTS = tuple(int(v) for v in np.cumsum(SPLITS)[:-1])
D_IN = sum(SPLITS)

kernel_name = "hybrid_pool_conv_gmlp_moe_stream_step"


def layer_norm(x, g, b):
    xf = x.astype(jnp.float32)
    mu = xf.mean(-1, keepdims=True)
    var = jnp.square(xf - mu).mean(-1, keepdims=True)
    return ((xf - mu) * lax.rsqrt(var + LN_EPS) * g + b).astype(x.dtype)


def pool_mixer(p, state, pos0, w, scale):
    Bn, T, _ = p.shape
    ext = jnp.concatenate([state.astype(p.dtype), p], axis=1)
    cs = jnp.pad(jnp.cumsum(ext.astype(jnp.float32), axis=1), ((0, 0), (1, 0), (0, 0)))
    end = cs[:, POOL_STATE + 1:]
    pos = pos0 + jnp.arange(T)
    groups = []
    for g, win in enumerate(POOL_WINDOWS):
        sl = slice(g * POOL_GROUP_DIM, (g + 1) * POOL_GROUP_DIM)
        s = end[..., sl] - cs[:, POOL_STATE + 1 - win: POOL_STATE + 1 - win + T, sl]
        cnt = jnp.minimum(pos + 1, win).astype(jnp.float32)[None, :, None]
        groups.append(s / cnt)
    mixed = jnp.concatenate(groups, axis=-1).astype(p.dtype) - p
    mixed = mixed.reshape(Bn, T, POOL_GROUPS, POOL_GROUP_DIM)
    out = jnp.einsum('btgc,gcd->btgd', mixed, w).reshape(Bn, T, D_BRANCH) * scale
    return out, ext[:, -POOL_STATE:]


def conv_mixer(xc, bg, cg, state, conv_w):
    T = xc.shape[1]
    z = cg * xc
    ext = jnp.concatenate([state.astype(z.dtype), z], axis=1)
    y = sum(ext[:, k:k + T] * conv_w[k] for k in range(CONV_WIDTH))
    return bg * y, ext[:, -(CONV_WIDTH - 1):]


def gmlp_mixer(u, v, ln_g, ln_b, ws, bs):
    vn = layer_norm(v, ln_g, ln_b)
    Bn, T, _ = vn.shape
    n = -(-T // GM_CHUNK)
    Tp = n * GM_CHUNK
    vp = jnp.pad(vn, ((0, 0), (0, Tp - T), (0, 0))).reshape(Bn, n, GM_CHUNK, GM_HEADS, GM_HEAD_DIM)
    mask = jnp.tril(jnp.ones((GM_CHUNK, GM_CHUNK), dtype=bool))
    wm = jnp.where(mask[None], ws, 0)
    s = jnp.einsum('hts,bnshc->bnthc', wm, vp) + bs.T[None, None, :, :, None]
    s = s.reshape(Bn, Tp, D_BRANCH)[:, :T]
    return u * s, vn


def moe(x, router_w, router_b, w_up, b_up, w_down, b_down):
    Bn, T, D = x.shape
    xt = x.reshape(-1, D)
    N = xt.shape[0]
    logits = (xt @ router_w + router_b).astype(jnp.float32)
    top_val, top_idx = lax.top_k(logits, TOP_K)
    gates = jax.nn.softmax(top_val, axis=-1)
    NK = N * TOP_K
    flat_e = top_idx.reshape(NK)
    order = jnp.argsort(flat_e, stable=True)
    sorted_e = flat_e[order]
    tok = order // TOP_K
    counts = jnp.bincount(flat_e, length=N_EXPERTS)
    padded = (counts + MOE_BLOCK - 1) // MOE_BLOCK * MOE_BLOCK
    pad_end = jnp.cumsum(padded)
    pad_start = pad_end - padded
    grp_start = jnp.cumsum(counts) - counts
    dest = pad_start[sorted_e] + jnp.arange(NK) - grp_start[sorted_e]
    n_blocks = -(-NK // MOE_BLOCK) + N_EXPERTS
    slot_tok = jnp.full((n_blocks * MOE_BLOCK,), N, dtype=jnp.int32).at[dest].set(tok.astype(jnp.int32))
    block_e = jnp.minimum(jnp.searchsorted(pad_end, jnp.arange(n_blocks) * MOE_BLOCK, side='right'), N_EXPERTS - 1)
    x_pad = jnp.concatenate([xt, jnp.zeros((1, D), xt.dtype)], axis=0)
    xs = x_pad[slot_tok].reshape(n_blocks, MOE_BLOCK, D)

    def expert_block(args):
        xb, e = args
        h = xb @ w_up[e] + b_up[e]
        glu = jnp.minimum(h[:, :D_FF], SWIGLU_LIMIT)
        lin = jnp.clip(h[:, D_FF:], -SWIGLU_LIMIT, SWIGLU_LIMIT)
        a = glu * jax.nn.sigmoid(SWIGLU_ALPHA * glu) * (lin + 1)
        return a @ w_down[e] + b_down[e]

    ys = lax.map(expert_block, (xs, block_e))
    y_slot = ys.reshape(-1, D)[dest]
    w_slot = gates.reshape(NK)[order].astype(x.dtype)
    out = jax.ops.segment_sum(y_slot * w_slot[:, None], tok, num_segments=N)
    return out.reshape(Bn, T, D)


def trunk(x, pool_state, conv_state, pos0, return_v, weights):
    (ln_in_g, ln_in_b, w_in, b_in, pool_w, pool_scale, conv_w, gm_ln_g, gm_ln_b, gm_ws, gm_bs,
     w_branch, w_out, ln1_g, ln1_b, router_w, router_b, exp_w_up, exp_b_up, exp_w_down, exp_b_down,
     ln2_g, ln2_b) = weights
    Bn, T, _ = x.shape
    h = layer_norm(x, ln_in_g, ln_in_b)
    new_pool, new_conv, new_v = [], [], []
    for l in range(DEPTH):
        proj = h @ w_in[l] + b_in[l]
        gate_cols, p, xc, bg, cg, u, v = jnp.split(proj, SPLIT_POINTS, axis=-1)
        a_out, ps = pool_mixer(p, pool_state[l], pos0, pool_w[l], pool_scale[l])
        c_out, cs = conv_mixer(xc, bg, cg, conv_state[l], conv_w[l])
        g_out, vn = gmlp_mixer(u, v, gm_ln_g[l], gm_ln_b[l], gm_ws[l], gm_bs[l])
        branches = jnp.stack([a_out, c_out, g_out], axis=2)
        br = jnp.einsum('btnc,ncd->btnd', branches, w_branch[l])
        gates = jax.nn.sigmoid(gate_cols).reshape(Bn, T, N_BRANCH, D_MODEL)
        merged = (gates * br).sum(axis=2)
        h = layer_norm(DN_ALPHA * h + merged @ w_out[l], ln1_g[l], ln1_b[l])
        h = layer_norm(DN_ALPHA * h + moe(h, router_w[l], router_b[l], exp_w_up[l], exp_b_up[l],
                                           exp_w_down[l], exp_b_down[l]), ln2_g[l], ln2_b[l])
        new_pool.append(ps)
        new_conv.append(cs)
        if return_v:
            new_v.append(vn)
    v_rows = jnp.stack(new_v) if return_v else None
    return h, jnp.stack(new_pool), jnp.stack(new_conv), v_rows


def setup_inputs(seed: int = 0) -> dict:
    key = jax.random.key(seed)
    ks = jax.random.split(key, 28)
    f32 = jnp.float32
    nrm = lambda k, shape, s: jax.random.normal(k, shape, f32) * s
    gain = lambda k, shape: 1.0 + 0.05 * jax.random.normal(k, shape, f32)
    return {
        "x_prompt": jax.random.normal(ks[0], (BATCH, SEQ, D_MODEL), f32),
        "x_sample": jax.random.normal(ks[1], (DEC_BATCH, DEC_SEQ, D_MODEL), f32),
        "state_pool": nrm(ks[2], (DEPTH, DEC_BATCH, POOL_STATE, D_BRANCH), 1.0),
        "state_conv": nrm(ks[3], (DEPTH, DEC_BATCH, CONV_WIDTH - 1, D_BRANCH), 1.0),
        "ln_in_g": gain(ks[4], (D_MODEL,)),
        "ln_in_b": nrm(ks[5], (D_MODEL,), 0.02),
        "w_in": nrm(ks[6], (DEPTH, D_MODEL, D_IN), D_MODEL ** -0.5),
        "b_in": nrm(ks[7], (DEPTH, D_IN), 0.02),
        "pool_w": nrm(ks[8], (DEPTH, POOL_GROUPS, POOL_GROUP_DIM, POOL_GROUP_DIM), POOL_GROUP_DIM ** -0.5),
        "pool_scale": gain(ks[9], (DEPTH, D_BRANCH)),
        "conv_w": nrm(ks[10], (DEPTH, CONV_WIDTH, D_BRANCH), CONV_WIDTH ** -0.5),
        "gm_ln_g": gain(ks[11], (DEPTH, D_BRANCH)),
        "gm_ln_b": nrm(ks[12], (DEPTH, D_BRANCH), 0.02),
        "gm_ws": nrm(ks[13], (DEPTH, GM_HEADS, GM_CHUNK, GM_CHUNK), GM_CHUNK ** -0.5),
        "gm_bs": 1.0 + nrm(ks[14], (DEPTH, GM_HEADS, GM_CHUNK), 0.1),
        "w_branch": nrm(ks[15], (DEPTH, N_BRANCH, D_BRANCH, D_MODEL), DN_BETA * D_BRANCH ** -0.5),
        "w_out": nrm(ks[16], (DEPTH, D_MODEL, D_MODEL), DN_BETA * D_MODEL ** -0.5),
        "ln1_g": gain(ks[17], (DEPTH, D_MODEL)),
        "ln1_b": nrm(ks[18], (DEPTH, D_MODEL), 0.02),
        "router_w": nrm(ks[19], (DEPTH, D_MODEL, N_EXPERTS), D_MODEL ** -0.5),
        "router_b": nrm(ks[20], (DEPTH, N_EXPERTS), 0.01),
        "exp_w_up": nrm(ks[21], (DEPTH, N_EXPERTS, D_MODEL, 2 * D_FF), DN_BETA * D_MODEL ** -0.5),
        "exp_b_up": nrm(ks[22], (DEPTH, N_EXPERTS, 2 * D_FF), 0.02),
        "exp_w_down": nrm(ks[23], (DEPTH, N_EXPERTS, D_FF, D_MODEL), DN_BETA * D_FF ** -0.5),
        "exp_b_down": nrm(ks[24], (DEPTH, N_EXPERTS, D_MODEL), 0.02),
        "ln2_g": gain(ks[25], (DEPTH, D_MODEL)),
        "ln2_b": nrm(ks[26], (DEPTH, D_MODEL), 0.02),
    }


def reference(x_prompt, x_sample, state_pool, state_conv, ln_in_g, ln_in_b, w_in, b_in, pool_w,
              pool_scale, conv_w, gm_ln_g, gm_ln_b, gm_ws, gm_bs, w_branch, w_out, ln1_g, ln1_b,
              router_w, router_b, exp_w_up, exp_b_up, exp_w_down, exp_b_down, ln2_g, ln2_b):
    weights = (ln_in_g, ln_in_b, w_in, b_in, pool_w, pool_scale, conv_w, gm_ln_g, gm_ln_b, gm_ws, gm_bs,
               w_branch, w_out, ln1_g, ln1_b, router_w, router_b, exp_w_up, exp_b_up, exp_w_down,
               exp_b_down, ln2_g, ln2_b)
    bp = x_prompt.shape[0]
    zero_pool = jnp.zeros((DEPTH, bp, POOL_STATE, D_BRANCH), x_prompt.dtype)
    zero_conv = jnp.zeros((DEPTH, bp, CONV_WIDTH - 1, D_BRANCH), x_prompt.dtype)
    y_prompt, new_pool_prompt, new_conv_prompt, _ = trunk(x_prompt, zero_pool, zero_conv, 0, False, weights)
    y_sample, new_pool_sample, new_conv_sample, gm_v_sample = trunk(
        x_sample, state_pool, state_conv, PAST_LEN, True, weights)
    return (y_prompt, y_sample, new_pool_prompt, new_conv_prompt, new_pool_sample, new_conv_sample, gm_v_sample)
```

```python
import functools

import jax
import jax.numpy as jnp
from jax import lax
from jax.experimental import pallas as pl
from jax.experimental.pallas import tpu as pltpu

F32 = jnp.float32
BF16 = jnp.bfloat16

POOL_WINDOWS = (2, 4, 8, 16)
POOL_HALO = 16
CONV_WIDTH = 3
CONV_TAIL = 8
GM_CHUNK = 128
GM_HEADS = 8
N_BRANCH = 3
TOP_K = 4
SWIGLU_LIMIT = 7.0
SWIGLU_ALPHA = 1.702
LN_EPS = 1e-5
LANES = 128

ROW_TILE = 256
MOE_BLOCK = 512
MOE_FF_TILE = 512
VMEM_LIMIT = 56 * 1024 * 1024


def _params(sem, vmem=VMEM_LIMIT):
    return pltpu.CompilerParams(dimension_semantics=sem, vmem_limit_bytes=vmem)


def _layer_norm(x, g, b):
    mu = jnp.mean(x, axis=-1, keepdims=True)
    xc = x - mu
    var = jnp.mean(xc * xc, axis=-1, keepdims=True)
    return xc * lax.rsqrt(var + LN_EPS) * g + b


def _resident(shape, index_map):
    return pl.BlockSpec(shape, index_map, pipeline_mode=pl.Buffered(1))


def _ln_in_kernel(xp_ref, xs_ref, g_ref, b_ref, h32_ref, h16_ref, *, n_prompt_tiles):
    i = pl.program_id(0)
    x = jnp.where(i < n_prompt_tiles, xp_ref[...], xs_ref[...])
    y = _layer_norm(x, g_ref[...], b_ref[...])
    h32_ref[...] = y
    h16_ref[...] = y.astype(BF16)


def _ln_in(xp, xs, g, b):
    n_p, d = xp.shape
    n_s = xs.shape[0]
    tp, ts = n_p // ROW_TILE, n_s // ROW_TILE
    n = n_p + n_s
    return pl.pallas_call(
        functools.partial(_ln_in_kernel, n_prompt_tiles=tp),
        grid=(tp + ts,),
        in_specs=[
            pl.BlockSpec((ROW_TILE, d), lambda i: (jnp.minimum(i, tp - 1), 0)),
            pl.BlockSpec((ROW_TILE, d), lambda i: (jnp.maximum(i - tp, 0), 0)),
            pl.BlockSpec((1, d), lambda i: (0, 0)),
            pl.BlockSpec((1, d), lambda i: (0, 0)),
        ],
        out_specs=[pl.BlockSpec((ROW_TILE, d), lambda i: (i, 0)),
                   pl.BlockSpec((ROW_TILE, d), lambda i: (i, 0))],
        out_shape=[jax.ShapeDtypeStruct((n, d), F32), jax.ShapeDtypeStruct((n, d), BF16)],
        compiler_params=_params(("arbitrary",)),
        name="ln_in",
    )(xp, xs, g.reshape(1, d), b.reshape(1, d))


def _proj_kernel(x_ref, w_ref, b_ref, o_ref):
    o_ref[...] = jnp.dot(x_ref[...], w_ref[...], preferred_element_type=F32) + b_ref[...]


def _pick_tile(n, candidates):
    for c in candidates:
        if n % c == 0:
            return c
    raise ValueError(f"no tile in {candidates} divides {n}")


def _proj(h16, w16, b):
    n, d = h16.shape
    d_in = w16.shape[1]
    bm = _pick_tile(n, (1280, 1024, 512, 256))
    bn = 1024
    return pl.pallas_call(
        _proj_kernel,
        grid=(n // bm, d_in // bn),
        in_specs=[
            pl.BlockSpec((bm, d), lambda i, j: (i, 0)),
            pl.BlockSpec((d, bn), lambda i, j: (0, j)),
            pl.BlockSpec((1, bn), lambda i, j: (0, j)),
        ],
        out_specs=pl.BlockSpec((bm, bn), lambda i, j: (i, j)),
        out_shape=jax.ShapeDtypeStruct((n, d_in), F32),
        compiler_params=_params(("arbitrary", "arbitrary")),
        name="proj",
    )(h16, w16, b.reshape(1, d_in))


def _mixer_kernel(gate_ref, p_ref, xc_ref, bg_ref, cg_ref, u_ref, v_ref, hp_ref, hxc_ref, hcg_ref,
                  sp_ref, sc_ref, poolw_ref, pscale_ref, convw_ref, lng_ref, lnb_ref, ws_ref, bst_ref,
                  wbr_ref, *rest, tt, pos0, want_v):
    if want_v:
        merged_ref, ptail_ref, ctail_ref, vn_ref, extp_ref, extz_ref, vpad_ref = rest
    else:
        merged_ref, ptail_ref, ctail_ref, extp_ref, extz_ref, vpad_ref = rest
        vn_ref = None
    i = pl.program_id(1)
    first = i == 0
    d_br = p_ref.shape[1]
    d_model = merged_ref.shape[1]
    gdim = d_br // len(POOL_WINDOWS)

    p = p_ref[...]
    extp_ref[0:POOL_HALO, :] = jnp.where(first, sp_ref[0], hp_ref[...])
    extp_ref[POOL_HALO:POOL_HALO + tt, :] = p
    pos = lax.broadcasted_iota(jnp.int32, (tt, 1), 0) + (i * tt + pos0)
    a_parts = []
    for g, win in enumerate(POOL_WINDOWS):
        cols = slice(g * gdim, (g + 1) * gdim)
        s = p[:, cols]
        for j in range(1, win):
            s = s + extp_ref[POOL_HALO - j:POOL_HALO - j + tt, cols]
        cnt = jnp.minimum(pos + 1, win).astype(F32)
        mixed = s / cnt - p[:, cols]
        a_parts.append(jnp.dot(mixed.astype(BF16), poolw_ref[g], preferred_element_type=F32))
    a_out = jnp.concatenate(a_parts, axis=-1) * pscale_ref[...]

    z = cg_ref[...] * xc_ref[...]
    extz_ref[0:POOL_HALO, :] = jnp.where(first, sc_ref[0], hcg_ref[...] * hxc_ref[...])
    extz_ref[POOL_HALO:POOL_HALO + tt, :] = z
    y = z * convw_ref[CONV_WIDTH - 1:CONV_WIDTH, :]
    for k in range(CONV_WIDTH - 1):
        off = POOL_HALO - (CONV_WIDTH - 1) + k
        y = y + extz_ref[off:off + tt, :] * convw_ref[k:k + 1, :]
    c_out = bg_ref[...] * y

    vn = _layer_norm(v_ref[...], lng_ref[...], lnb_ref[...])
    if want_v:
        vn_ref[...] = vn
    hd = d_br // GM_HEADS
    tri = (lax.broadcasted_iota(jnp.int32, (GM_CHUNK, GM_CHUNK), 0)
           >= lax.broadcasted_iota(jnp.int32, (GM_CHUNK, GM_CHUNK), 1))
    wms = [jnp.where(tri, ws_ref[h], jnp.zeros((), BF16)) for h in range(GM_HEADS)]
    s_chunks = []
    for c in range(max(tt // GM_CHUNK, 1)):
        if tt < GM_CHUNK:
            vpad_ref[...] = jnp.zeros(vpad_ref.shape, BF16)
            vpad_ref[0:tt, :] = vn.astype(BF16)
            vchunk = vpad_ref[...]
        else:
            vchunk = vn[c * GM_CHUNK:(c + 1) * GM_CHUNK, :].astype(BF16)
        heads = []
        for h in range(GM_HEADS):
            sh = jnp.dot(wms[h], vchunk[:, h * hd:(h + 1) * hd], preferred_element_type=F32)
            heads.append(sh + bst_ref[:, h:h + 1])
        s_chunks.append(jnp.concatenate(heads, axis=-1)[:min(tt, GM_CHUNK), :])
    s_all = s_chunks[0] if len(s_chunks) == 1 else jnp.concatenate(s_chunks, axis=0)
    g_out = u_ref[...] * s_all

    merged = None
    for n, br in enumerate((a_out, c_out, g_out)):
        t = jnp.dot(br.astype(BF16), wbr_ref[n], preferred_element_type=F32)
        t = jax.nn.sigmoid(gate_ref[:, n * d_model:(n + 1) * d_model]) * t
        merged = t if merged is None else merged + t
    merged_ref[...] = merged
    ptail_ref[0] = p[tt - POOL_HALO:tt, :]
    ctail_ref[0] = z[tt - CONV_TAIL:tt, :]


def _mixer(proj, row_off, nb, t, tt, pos0, sp16, sc16, lw, want_v):
    d_in = proj.shape[1]
    d_br = sp16.shape[2]
    d_model = lw["w_branch"].shape[2]
    gate_w = N_BRANCH * d_model
    col0 = gate_w // d_br
    nt = t // tt
    base = row_off // tt

    def row(b, i):
        return base + b * nt + i

    def halo(b, i):
        return jnp.maximum((row_off + b * t + i * tt) // POOL_HALO - 1, 0)

    in_specs = [pl.BlockSpec((tt, gate_w), lambda b, i: (row(b, i), 0))]
    in_specs += [pl.BlockSpec((tt, d_br), functools.partial(lambda b, i, c: (row(b, i), col0 + c), c=c))
                 for c in range(6)]
    in_specs += [pl.BlockSpec((POOL_HALO, d_br), functools.partial(lambda b, i, c: (halo(b, i), col0 + c), c=c))
                 for c in (0, 1, 3)]
    in_specs += [pl.BlockSpec((1, POOL_HALO, d_br), lambda b, i: (b, 0, 0)),
                 pl.BlockSpec((1, POOL_HALO, d_br), lambda b, i: (b, 0, 0))]
    gdim = d_br // len(POOL_WINDOWS)
    in_specs += [
        _resident((len(POOL_WINDOWS), gdim, gdim), lambda b, i: (0, 0, 0)),
        _resident((1, d_br), lambda b, i: (0, 0)),
        _resident((CONV_WIDTH, d_br), lambda b, i: (0, 0)),
        _resident((1, d_br), lambda b, i: (0, 0)),
        _resident((1, d_br), lambda b, i: (0, 0)),
        _resident((GM_HEADS, GM_CHUNK, GM_CHUNK), lambda b, i: (0, 0, 0)),
        _resident((GM_CHUNK, GM_HEADS), lambda b, i: (0, 0)),
        _resident((N_BRANCH, d_br, d_model), lambda b, i: (0, 0, 0)),
    ]
    out_specs = [pl.BlockSpec((tt, d_model), lambda b, i: (b * nt + i, 0)),
                 pl.BlockSpec((1, POOL_HALO, d_br), lambda b, i: (b, 0, 0)),
                 pl.BlockSpec((1, CONV_TAIL, d_br), lambda b, i: (b, 0, 0))]
    out_shape = [jax.ShapeDtypeStruct((nb * t, d_model), F32),
                 jax.ShapeDtypeStruct((nb, POOL_HALO, d_br), F32),
                 jax.ShapeDtypeStruct((nb, CONV_TAIL, d_br), F32)]
    if want_v:
        out_specs.append(pl.BlockSpec((tt, d_br), lambda b, i: (b * nt + i, 0)))
        out_shape.append(jax.ShapeDtypeStruct((nb * t, d_br), F32))
    return pl.pallas_call(
        functools.partial(_mixer_kernel, tt=tt, pos0=pos0, want_v=want_v),
        grid=(nb, nt),
        in_specs=in_specs,
        out_specs=out_specs,
        out_shape=out_shape,
        scratch_shapes=[pltpu.VMEM((POOL_HALO + tt, d_br), F32),
                        pltpu.VMEM((POOL_HALO + tt, d_br), F32),
                        pltpu.VMEM((GM_CHUNK, d_br), BF16)],
        compiler_params=_params(("arbitrary", "arbitrary")),
        name=f"mixer_t{t}",
    )(proj, proj, proj, proj, proj, proj, proj, proj, proj, proj, sp16, sc16,
      lw["pool_w"], lw["pool_scale"], lw["conv_w"], lw["gm_ln_g"], lw["gm_ln_b"], lw["gm_ws"], lw["gm_bst"],
      lw["w_branch"])


def _outproj_kernel(mp_ref, ms_ref, h_ref, w_ref, g_ref, b_ref, rwh_ref, rwl_ref, rb_ref,
                    h32_ref, route_ref, gates_ref, counts_ref, carry_ref, *, n_prompt_tiles, alpha, n_experts):
    i = pl.program_id(0)
    tm = h_ref.shape[0]

    @pl.when(i == 0)
    def _():
        carry_ref[...] = jnp.zeros(carry_ref.shape, F32)

    m = jnp.where(i < n_prompt_tiles, mp_ref[...], ms_ref[...])
    y = alpha * h_ref[...] + jnp.dot(m.astype(BF16), w_ref[...], preferred_element_type=F32)
    h1 = _layer_norm(y, g_ref[...], b_ref[...])
    h32_ref[...] = h1

    hh = h1.astype(BF16)
    hl = (h1 - hh.astype(F32)).astype(BF16)
    logits = (jnp.dot(hh, rwh_ref[...], preferred_element_type=F32)
              + jnp.dot(hl, rwh_ref[...], preferred_element_type=F32)
              + jnp.dot(hh, rwl_ref[...], preferred_element_type=F32)) + rb_ref[...]
    lane = lax.broadcasted_iota(jnp.int32, (tm, LANES), 1)
    lane_f = lane.astype(F32)
    neg = jnp.float32(-jnp.inf)
    cur = jnp.where(lane < n_experts, logits, neg)
    vals, idxs = [], []
    for _ in range(TOP_K):
        mx = jnp.max(cur, axis=-1, keepdims=True)
        ik = jnp.min(jnp.where(cur == mx, lane_f, float(LANES)), axis=-1, keepdims=True).astype(jnp.int32)
        vals.append(mx)
        idxs.append(ik)
        cur = jnp.where(lane == ik, neg, cur)
    exps = [jnp.exp(v - vals[0]) for v in vals]
    denom = exps[0]
    for e in exps[1:]:
        denom = denom + e

    onehot = jnp.zeros((tm, LANES), F32)
    for ik in idxs:
        onehot = onehot + (lane == ik).astype(F32)
    below = (lax.broadcasted_iota(jnp.int32, (tm, tm), 0)
             > lax.broadcasted_iota(jnp.int32, (tm, tm), 1)).astype(BF16)
    prefix = jnp.dot(below, onehot.astype(BF16), preferred_element_type=F32) + carry_ref[...]
    route = jnp.zeros((tm, LANES), jnp.int32)
    gates = jnp.zeros((tm, LANES), F32)
    for k in range(TOP_K):
        rank = jnp.sum(jnp.where(lane == idxs[k], prefix, 0.0), axis=-1, keepdims=True).astype(jnp.int32)
        route = jnp.where(lane == k, idxs[k], route)
        route = jnp.where(lane == TOP_K + k, rank, route)
        gates = jnp.where(lane == k, exps[k] / denom, gates)
    route_ref[...] = route
    gates_ref[...] = gates
    carry_ref[...] = carry_ref[...] + jnp.sum(onehot, axis=0, keepdims=True)
    counts_ref[...] = jnp.broadcast_to(carry_ref[...], counts_ref.shape)


def _outproj(mp, ms, h32, lw, alpha, n_experts):
    n, d = h32.shape
    tp, ts = mp.shape[0] // ROW_TILE, ms.shape[0] // ROW_TILE
    return pl.pallas_call(
        functools.partial(_outproj_kernel, n_prompt_tiles=tp, alpha=alpha, n_experts=n_experts),
        grid=(tp + ts,),
        in_specs=[
            pl.BlockSpec((ROW_TILE, d), lambda i: (jnp.minimum(i, tp - 1), 0)),
            pl.BlockSpec((ROW_TILE, d), lambda i: (jnp.maximum(i - tp, 0), 0)),
            pl.BlockSpec((ROW_TILE, d), lambda i: (i, 0)),
            _resident((d, d), lambda i: (0, 0)),
            _resident((1, d), lambda i: (0, 0)),
            _resident((1, d), lambda i: (0, 0)),
            _resident((d, LANES), lambda i: (0, 0)),
            _resident((d, LANES), lambda i: (0, 0)),
            _resident((1, LANES), lambda i: (0, 0)),
        ],
        out_specs=[pl.BlockSpec((ROW_TILE, d), lambda i: (i, 0)),
                   pl.BlockSpec((ROW_TILE, LANES), lambda i: (i, 0)),
                   pl.BlockSpec((ROW_TILE, LANES), lambda i: (i, 0)),
                   pl.BlockSpec((8, LANES), lambda i: (0, 0))],
        out_shape=[jax.ShapeDtypeStruct((n, d), F32),
                   jax.ShapeDtypeStruct((n, LANES), jnp.int32),
                   jax.ShapeDtypeStruct((n, LANES), F32),
                   jax.ShapeDtypeStruct((8, LANES), F32)],
        scratch_shapes=[pltpu.VMEM((1, LANES), F32)],
        compiler_params=_params(("arbitrary",)),
        name="outproj_router",
    )(mp, ms, h32, lw["w_out"], lw["ln1_g"], lw["ln1_b"], lw["router_wh"], lw["router_wl"], lw["router_b"])


def _dispatch_kernel(dest_ref, h_ref, xs_in_ref, xs_ref, sem):
    del xs_in_ref
    tm = h_ref.shape[0]

    def row_copy(r, k):
        return pltpu.make_async_copy(h_ref.at[pl.ds(r, 1)], xs_ref.at[pl.ds(dest_ref[r * TOP_K + k], 1)], sem)

    def start(r, c):
        for k in range(TOP_K):
            row_copy(r, k).start()
        return c

    def wait(r, c):
        for k in range(TOP_K):
            row_copy(r, k).wait()
        return c

    lax.fori_loop(0, tm, start, 0)
    lax.fori_loop(0, tm, wait, 0)


def _dispatch(h32, dest_flat, xs_zero):
    n, d = h32.shape
    return pl.pallas_call(
        _dispatch_kernel,
        grid=(n // ROW_TILE,),
        in_specs=[
            pl.BlockSpec((ROW_TILE * TOP_K,), lambda i: (i,), memory_space=pltpu.SMEM),
            pl.BlockSpec((ROW_TILE, d), lambda i: (i, 0)),
            pl.BlockSpec(memory_space=pl.ANY),
        ],
        out_specs=pl.BlockSpec(memory_space=pl.ANY),
        out_shape=jax.ShapeDtypeStruct(xs_zero.shape, xs_zero.dtype),
        scratch_shapes=[pltpu.SemaphoreType.DMA(())],
        input_output_aliases={2: 0},
        compiler_params=_params(("arbitrary",)),
        name="dispatch",
    )(dest_flat, h32, xs_zero)


def _experts_kernel(be_ref, br_ref, nv_ref, x_ref, wg_ref, wl_ref, bg_ref, bl_ref, wd_ref, bd_ref, o_ref, x16_ref):
    del be_ref, br_ref
    b = pl.program_id(0)
    f = pl.program_id(1)

    @pl.when(nv_ref[b] > 0)
    def _():
        @pl.when(f == 0)
        def _():
            x16_ref[...] = x_ref[...].astype(BF16)

        x = x16_ref[...]
        hg = jnp.dot(x, wg_ref[0], preferred_element_type=F32) + bg_ref[0]
        hl = jnp.dot(x, wl_ref[0], preferred_element_type=F32) + bl_ref[0]
        glu = jnp.minimum(hg, SWIGLU_LIMIT)
        lin = jnp.clip(hl, -SWIGLU_LIMIT, SWIGLU_LIMIT)
        a = glu * jax.nn.sigmoid(SWIGLU_ALPHA * glu) * (lin + 1.0)
        y = jnp.dot(a.astype(BF16), wd_ref[0], preferred_element_type=F32)

        @pl.when(f == 0)
        def _():
            o_ref[...] = y + bd_ref[0]

        @pl.when(f > 0)
        def _():
            o_ref[...] += y

    @pl.when(jnp.logical_and(nv_ref[b] == 0, f == 0))
    def _():
        o_ref[...] = jnp.zeros(o_ref.shape, F32)


def _experts(xs, block_e, block_row, block_valid, lw):
    rows, d = xs.shape
    n_exp, _, d_ff2 = lw["exp_w_up"].shape
    d_ff = d_ff2 // 2
    nf = d_ff // MOE_FF_TILE
    n_blocks = rows // MOE_BLOCK
    grid_spec = pltpu.PrefetchScalarGridSpec(
        num_scalar_prefetch=3,
        grid=(n_blocks, nf),
        in_specs=[
            pl.BlockSpec((MOE_BLOCK, d), lambda b, f, be, br, nv: (br[b], 0)),
            pl.BlockSpec((1, d, MOE_FF_TILE), lambda b, f, be, br, nv: (be[b], 0, f)),
            pl.BlockSpec((1, d, MOE_FF_TILE), lambda b, f, be, br, nv: (be[b], 0, nf + f)),
            pl.BlockSpec((1, 1, MOE_FF_TILE), lambda b, f, be, br, nv: (be[b], 0, f)),
            pl.BlockSpec((1, 1, MOE_FF_TILE), lambda b, f, be, br, nv: (be[b], 0, nf + f)),
            pl.BlockSpec((1, MOE_FF_TILE, d), lambda b, f, be, br, nv: (be[b], f, 0)),
            pl.BlockSpec((1, 1, d), lambda b, f, be, br, nv: (be[b], 0, 0)),
        ],
        out_specs=pl.BlockSpec((MOE_BLOCK, d), lambda b, f, be, br, nv: (b, 0)),
        scratch_shapes=[pltpu.VMEM((MOE_BLOCK, d), BF16)],
    )
    return pl.pallas_call(
        _experts_kernel,
        grid_spec=grid_spec,
        out_shape=jax.ShapeDtypeStruct((rows, d), F32),
        compiler_params=_params(("arbitrary", "arbitrary")),
        name="experts",
    )(block_e, block_row, block_valid, xs, lw["exp_w_up"], lw["exp_w_up"], lw["exp_b_up"], lw["exp_b_up"],
      lw["exp_w_down"], lw["exp_b_down"])


def _combine_kernel(dest_ref, ys_ref, gates_ref, h_ref, g_ref, b_ref, h32_ref, h16_ref, buf_ref, sem, *, alpha):
    tm = h_ref.shape[0]

    def row_copy(r, k):
        return pltpu.make_async_copy(ys_ref.at[pl.ds(dest_ref[r * TOP_K + k], 1)],
                                     buf_ref.at[k, pl.ds(r, 1)], sem)

    def start(r, c):
        for k in range(TOP_K):
            row_copy(r, k).start()
        return c

    def wait(r, c):
        for k in range(TOP_K):
            row_copy(r, k).wait()
        return c

    lax.fori_loop(0, tm, start, 0)
    lax.fori_loop(0, tm, wait, 0)
    gates = gates_ref[...]
    moe = None
    for k in range(TOP_K):
        t = buf_ref[k] * gates[:, k:k + 1]
        moe = t if moe is None else moe + t
    y = _layer_norm(alpha * h_ref[...] + moe, g_ref[...], b_ref[...])
    h32_ref[...] = y
    h16_ref[...] = y.astype(BF16)


def _combine(ys, dest_flat, gates, h32, g, b, alpha):
    n, d = h32.shape
    return pl.pallas_call(
        functools.partial(_combine_kernel, alpha=alpha),
        grid=(n // ROW_TILE,),
        in_specs=[
            pl.BlockSpec((ROW_TILE * TOP_K,), lambda i: (i,), memory_space=pltpu.SMEM),
            pl.BlockSpec(memory_space=pl.ANY),
            pl.BlockSpec((ROW_TILE, LANES), lambda i: (i, 0)),
            pl.BlockSpec((ROW_TILE, d), lambda i: (i, 0)),
            pl.BlockSpec((1, d), lambda i: (0, 0)),
            pl.BlockSpec((1, d), lambda i: (0, 0)),
        ],
        out_specs=[pl.BlockSpec((ROW_TILE, d), lambda i: (i, 0)),
                   pl.BlockSpec((ROW_TILE, d), lambda i: (i, 0))],
        out_shape=[jax.ShapeDtypeStruct((n, d), F32), jax.ShapeDtypeStruct((n, d), BF16)],
        scratch_shapes=[pltpu.VMEM((TOP_K, ROW_TILE, d), F32), pltpu.SemaphoreType.DMA(())],
        compiler_params=_params(("arbitrary",)),
        name="combine",
    )(dest_flat, ys, gates, h32, g, b)


def _routing_tables(route, counts_f, n_experts, n_blocks):
    idx = route[:, :TOP_K]
    rank = route[:, TOP_K:2 * TOP_K]
    counts = counts_f[0, :n_experts].astype(jnp.int32)
    padded = (counts + MOE_BLOCK - 1) // MOE_BLOCK * MOE_BLOCK
    pad_end = jnp.cumsum(padded)
    pad_start = pad_end - padded
    dest = (pad_start[idx] + rank).reshape(-1).astype(jnp.int32)
    n_used = pad_end[-1] // MOE_BLOCK
    blk = jnp.minimum(jnp.arange(n_blocks, dtype=jnp.int32), n_used - 1)
    block_e = jnp.minimum(jnp.searchsorted(pad_end, blk * MOE_BLOCK, side="right"), n_experts - 1).astype(jnp.int32)
    in_use = jnp.arange(n_blocks, dtype=jnp.int32) < n_used
    valid = jnp.clip(counts[block_e] - (blk * MOE_BLOCK - pad_start[block_e]), 0, MOE_BLOCK)
    valid = jnp.where(in_use, valid, 0).astype(jnp.int32)
    return dest, block_e, blk.astype(jnp.int32), valid


def _pad_state(state, rows):
    return jnp.pad(state, ((0, 0), (rows - state.shape[1], 0), (0, 0)))


def kernel(x_prompt, x_sample, state_pool, state_conv, ln_in_g, ln_in_b, w_in, b_in, pool_w, pool_scale, conv_w,
           gm_ln_g, gm_ln_b, gm_ws, gm_bs, w_branch, w_out, ln1_g, ln1_b, router_w, router_b, exp_w_up,
           exp_b_up, exp_w_down, exp_b_down, ln2_g, ln2_b):
    bp, seq, d = x_prompt.shape
    bs, dseq, _ = x_sample.shape
    depth = w_in.shape[0]
    d_br = pool_scale.shape[1]
    n_experts = router_w.shape[2]
    past_len = 4096
    alpha = float((2 * depth) ** 0.25)
    n_p, n_s = bp * seq, bs * dseq
    n = n_p + n_s
    n_blocks = (n * TOP_K) // MOE_BLOCK + n_experts
    pool_state = POOL_WINDOWS[-1] - 1

    h32, h16 = _ln_in(x_prompt.reshape(n_p, d), x_sample.reshape(n_s, d), ln_in_g, ln_in_b)
    zero_pool = jnp.zeros((bp, POOL_HALO, d_br), F32)
    new_pool_p, new_conv_p, new_pool_s, new_conv_s, v_rows = [], [], [], [], []
    for l in range(depth):
        rw = jnp.pad(router_w[l], ((0, 0), (0, LANES - n_experts)))
        rwh = rw.astype(BF16)
        lw = dict(
            pool_w=pool_w[l].astype(BF16), pool_scale=pool_scale[l].reshape(1, d_br), conv_w=conv_w[l],
            gm_ln_g=gm_ln_g[l].reshape(1, d_br), gm_ln_b=gm_ln_b[l].reshape(1, d_br),
            gm_ws=gm_ws[l].astype(BF16), gm_bst=gm_bs[l].T, w_branch=w_branch[l].astype(BF16),
            w_out=w_out[l].astype(BF16), ln1_g=ln1_g[l].reshape(1, d), ln1_b=ln1_b[l].reshape(1, d),
            router_wh=rwh, router_wl=(rw - rwh.astype(F32)).astype(BF16),
            router_b=jnp.pad(router_b[l], (0, LANES - n_experts)).reshape(1, LANES),
            exp_w_up=exp_w_up[l].astype(BF16), exp_b_up=exp_b_up[l].reshape(n_experts, 1, -1),
            exp_w_down=exp_w_down[l].astype(BF16), exp_b_down=exp_b_down[l].reshape(n_experts, 1, d),
        )
        proj = _proj(h16, w_in[l].astype(BF16), b_in[l])
        mp, ptail_p, ctail_p = _mixer(proj, 0, bp, seq, GM_CHUNK, 0, zero_pool, zero_pool, lw, False)
        ms, ptail_s, ctail_s, vn = _mixer(proj, n_p, bs, dseq, dseq, past_len,
                                          _pad_state(state_pool[l], POOL_HALO), _pad_state(state_conv[l], POOL_HALO),
                                          lw, True)
        h1, route, gates, counts = _outproj(mp, ms, h32, lw, alpha, n_experts)
        dest, block_e, block_row, block_valid = _routing_tables(route, counts, n_experts, n_blocks)
        xs = _dispatch(h1, dest, jnp.zeros((n_blocks * MOE_BLOCK, d), F32))
        ys = _experts(xs, block_e, block_row, block_valid, lw)
        h32, h16 = _combine(ys, dest, gates, h1, ln2_g[l].reshape(1, d), ln2_b[l].reshape(1, d), alpha)
        new_pool_p.append(ptail_p[:, POOL_HALO - pool_state:, :])
        new_conv_p.append(ctail_p[:, CONV_TAIL - (CONV_WIDTH - 1):, :])
        new_pool_s.append(ptail_s[:, POOL_HALO - pool_state:, :])
        new_conv_s.append(ctail_s[:, CONV_TAIL - (CONV_WIDTH - 1):, :])
        v_rows.append(vn.reshape(bs, dseq, d_br))
    y_prompt = h32[:n_p].reshape(bp, seq, d)
    y_sample = h32[n_p:].reshape(bs, dseq, d)
    return (y_prompt, y_sample, jnp.stack(new_pool_p), jnp.stack(new_conv_p), jnp.stack(new_pool_s),
            jnp.stack(new_conv_s), jnp.stack(v_rows))
```

```python
import functools

import jax
import jax.numpy as jnp
from jax import lax
from jax.experimental import pallas as pl
from jax.experimental.pallas import tpu as pltpu

F32 = jnp.float32
BF16 = jnp.bfloat16
I32 = jnp.int32

POOL_WINDOWS = (2, 4, 8, 16)
POOL_HALO = 16
CONV_WIDTH = 3
CONV_TAIL = 8
GM_CHUNK = 128
GM_HEADS = 8
N_BRANCH = 3
TOP_K = 4
SWIGLU_LIMIT = 7.0
SWIGLU_ALPHA = 1.702
LN_EPS = 1e-5
PAST_LEN = 4096
LANES = 128

ROW_TILE = 256
MIX_TILE = 256
MOE_BLOCK = 1024
MOE_FF_TILE = 256
VMEM_LIMIT = 60 * 1024 * 1024
HIGH_HALF = -65536


def _params(sem, vmem=VMEM_LIMIT):
    return pltpu.CompilerParams(dimension_semantics=sem, vmem_limit_bytes=vmem)


def _layer_norm(x, g, b):
    mu = jnp.mean(x, axis=-1, keepdims=True)
    xc = x - mu
    var = jnp.mean(xc * xc, axis=-1, keepdims=True)
    return xc * lax.rsqrt(var + LN_EPS) * g + b


def _resident(shape, index_map):
    return pl.BlockSpec(shape, index_map, pipeline_mode=pl.Buffered(1))


def _pack_halves(y):
    half = y.shape[1] // 2
    bits = pltpu.bitcast(y.astype(BF16).astype(F32), I32)
    return lax.shift_right_logical(bits[:, :half], 16) | bits[:, half:]


def _unpack_halves(w):
    return pltpu.bitcast(w << 16, F32), pltpu.bitcast(w & HIGH_HALF, F32)


def _ln_in_kernel(xp_ref, xs_ref, g_ref, b_ref, h32_ref, h16_ref, *, n_prompt_tiles):
    i = pl.program_id(0)
    x = jnp.where(i < n_prompt_tiles, xp_ref[...], xs_ref[...])
    y = _layer_norm(x, g_ref[...], b_ref[...])
    h32_ref[...] = y
    h16_ref[...] = y.astype(BF16)


def _ln_in(xp, xs, g, b):
    n_p, d = xp.shape
    n_s = xs.shape[0]
    tp, ts = n_p // ROW_TILE, n_s // ROW_TILE
    n = n_p + n_s
    return pl.pallas_call(
        functools.partial(_ln_in_kernel, n_prompt_tiles=tp),
        grid=(tp + ts,),
        in_specs=[
            pl.BlockSpec((ROW_TILE, d), lambda i: (jnp.minimum(i, tp - 1), 0)),
            pl.BlockSpec((ROW_TILE, d), lambda i: (jnp.maximum(i - tp, 0), 0)),
            pl.BlockSpec((1, d), lambda i: (0, 0)),
            pl.BlockSpec((1, d), lambda i: (0, 0)),
        ],
        out_specs=[pl.BlockSpec((ROW_TILE, d), lambda i: (i, 0)),
                   pl.BlockSpec((ROW_TILE, d), lambda i: (i, 0))],
        out_shape=[jax.ShapeDtypeStruct((n, d), F32), jax.ShapeDtypeStruct((n, d), BF16)],
        compiler_params=_params(("arbitrary",)),
        name="ln_in",
    )(xp, xs, g.reshape(1, d), b.reshape(1, d))


def _proj_kernel(x_ref, w_ref, b_ref, o_ref):
    o_ref[...] = (jnp.dot(x_ref[...], w_ref[...], preferred_element_type=F32) + b_ref[...]).astype(o_ref.dtype)


def _pick_tile(n, candidates):
    for c in candidates:
        if n % c == 0:
            return c
    raise ValueError(f"no tile in {candidates} divides {n}")


def _proj(h16, w16, b, l):
    n, d = h16.shape
    d_in = w16.shape[2]
    bm = _pick_tile(n, (1280, 1024, 512, 256))
    bn = 1024
    return pl.pallas_call(
        _proj_kernel,
        grid=(n // bm, d_in // bn),
        in_specs=[
            pl.BlockSpec((bm, d), lambda i, j: (i, 0)),
            pl.BlockSpec((None, d, bn), lambda i, j: (l, 0, j)),
            pl.BlockSpec((None, 1, bn), lambda i, j: (l, 0, j)),
        ],
        out_specs=pl.BlockSpec((bm, bn), lambda i, j: (i, j)),
        out_shape=jax.ShapeDtypeStruct((n, d_in), BF16),
        compiler_params=_params(("arbitrary", "arbitrary")),
        name="proj",
    )(h16, w16, b)


def _mixer_kernel(gate_ref, p_ref, xc_ref, bg_ref, cg_ref, u_ref, v_ref, hp_ref, hxc_ref, hcg_ref,
                  sp_ref, sc_ref, poolw_ref, pscale_ref, convw_ref, lng_ref, lnb_ref, ws_ref, bst_ref,
                  wbr_ref, *rest, tt, pos0, want_v):
    if want_v:
        merged_ref, ptail_ref, ctail_ref, vn_ref, extp_ref, extz_ref, vpad_ref = rest
    else:
        merged_ref, ptail_ref, ctail_ref, extp_ref, extz_ref, vpad_ref = rest
        vn_ref = None
    i = pl.program_id(1)
    first = i == 0
    d_br = p_ref.shape[1]
    d_model = merged_ref.shape[1]
    gdim = d_br // len(POOL_WINDOWS)

    p = p_ref[...].astype(F32)
    extp_ref[0:POOL_HALO, :] = jnp.where(first, sp_ref[0], hp_ref[...].astype(F32))
    extp_ref[POOL_HALO:POOL_HALO + tt, :] = p
    pos = lax.broadcasted_iota(I32, (tt, 1), 0) + (i * tt + pos0)
    a_parts = []
    for g, win in enumerate(POOL_WINDOWS):
        cols = slice(g * gdim, (g + 1) * gdim)
        s = p[:, cols]
        for j in range(1, win):
            s = s + extp_ref[POOL_HALO - j:POOL_HALO - j + tt, cols]
        cnt = jnp.minimum(pos + 1, win).astype(F32)
        mixed = s / cnt - p[:, cols]
        a_parts.append(jnp.dot(mixed.astype(BF16), poolw_ref[g], preferred_element_type=F32))
    a_out = jnp.concatenate(a_parts, axis=-1) * pscale_ref[...]

    z = cg_ref[...].astype(F32) * xc_ref[...].astype(F32)
    extz_ref[0:POOL_HALO, :] = jnp.where(first, sc_ref[0], hcg_ref[...].astype(F32) * hxc_ref[...].astype(F32))
    extz_ref[POOL_HALO:POOL_HALO + tt, :] = z
    y = z * convw_ref[CONV_WIDTH - 1:CONV_WIDTH, :]
    for k in range(CONV_WIDTH - 1):
        off = POOL_HALO - (CONV_WIDTH - 1) + k
        y = y + extz_ref[off:off + tt, :] * convw_ref[k:k + 1, :]
    c_out = bg_ref[...].astype(F32) * y

    vn = _layer_norm(v_ref[...].astype(F32), lng_ref[...], lnb_ref[...])
    if want_v:
        vn_ref[...] = vn
    hd = d_br // GM_HEADS
    tri = (lax.broadcasted_iota(I32, (GM_CHUNK, GM_CHUNK), 0)
           >= lax.broadcasted_iota(I32, (GM_CHUNK, GM_CHUNK), 1))
    wms = [jnp.where(tri, ws_ref[h], jnp.zeros((), BF16)) for h in range(GM_HEADS)]
    s_chunks = []
    for c in range(max(tt // GM_CHUNK, 1)):
        if tt < GM_CHUNK:
            vpad_ref[...] = jnp.zeros(vpad_ref.shape, BF16)
            vpad_ref[0:tt, :] = vn.astype(BF16)
            vchunk = vpad_ref[...]
        else:
            vchunk = vn[c * GM_CHUNK:(c + 1) * GM_CHUNK, :].astype(BF16)
        heads = []
        for h in range(GM_HEADS):
            sh = jnp.dot(wms[h], vchunk[:, h * hd:(h + 1) * hd], preferred_element_type=F32)
            heads.append(sh + bst_ref[:, h:h + 1])
        s_chunks.append(jnp.concatenate(heads, axis=-1)[:min(tt, GM_CHUNK), :])
    s_all = s_chunks[0] if len(s_chunks) == 1 else jnp.concatenate(s_chunks, axis=0)
    g_out = u_ref[...].astype(F32) * s_all

    merged = None
    for n, br in enumerate((a_out, c_out, g_out)):
        t = jnp.dot(br.astype(BF16), wbr_ref[n], preferred_element_type=F32)
        t = jax.nn.sigmoid(gate_ref[:, n * d_model:(n + 1) * d_model].astype(F32)) * t
        merged = t if merged is None else merged + t
    merged_ref[...] = merged.astype(merged_ref.dtype)
    ptail_ref[0] = p[tt - POOL_HALO:tt, :]
    ctail_ref[0] = z[tt - CONV_TAIL:tt, :]


def _mixer(proj, row_off, nb, t, tt, pos0, sp16, sc16, lw, l, want_v):
    d_br = sp16.shape[2]
    d_model = lw["w_branch"].shape[3]
    gate_w = N_BRANCH * d_model
    col0 = gate_w // d_br
    nt = t // tt
    base = row_off // tt

    def row(b, i):
        return base + b * nt + i

    def halo(b, i):
        return jnp.maximum((row_off + b * t + i * tt) // POOL_HALO - 1, 0)

    in_specs = [pl.BlockSpec((tt, gate_w), lambda b, i: (row(b, i), 0))]
    in_specs += [pl.BlockSpec((tt, d_br), functools.partial(lambda b, i, c: (row(b, i), col0 + c), c=c))
                 for c in range(6)]
    in_specs += [pl.BlockSpec((POOL_HALO, d_br), functools.partial(lambda b, i, c: (halo(b, i), col0 + c), c=c))
                 for c in (0, 1, 3)]
    in_specs += [pl.BlockSpec((1, POOL_HALO, d_br), lambda b, i: (b, 0, 0)),
                 pl.BlockSpec((1, POOL_HALO, d_br), lambda b, i: (b, 0, 0))]
    gdim = d_br // len(POOL_WINDOWS)
    in_specs += [
        _resident((None, len(POOL_WINDOWS), gdim, gdim), lambda b, i: (l, 0, 0, 0)),
        _resident((None, 1, d_br), lambda b, i: (l, 0, 0)),
        _resident((None, CONV_WIDTH, d_br), lambda b, i: (l, 0, 0)),
        _resident((None, 1, d_br), lambda b, i: (l, 0, 0)),
        _resident((None, 1, d_br), lambda b, i: (l, 0, 0)),
        _resident((None, GM_HEADS, GM_CHUNK, GM_CHUNK), lambda b, i: (l, 0, 0, 0)),
        _resident((None, GM_CHUNK, GM_HEADS), lambda b, i: (l, 0, 0)),
        _resident((None, N_BRANCH, d_br, d_model), lambda b, i: (l, 0, 0, 0)),
    ]
    out_specs = [pl.BlockSpec((tt, d_model), lambda b, i: (b * nt + i, 0)),
                 pl.BlockSpec((1, POOL_HALO, d_br), lambda b, i: (b, 0, 0)),
                 pl.BlockSpec((1, CONV_TAIL, d_br), lambda b, i: (b, 0, 0))]
    out_shape = [jax.ShapeDtypeStruct((nb * t, d_model), BF16),
                 jax.ShapeDtypeStruct((nb, POOL_HALO, d_br), F32),
                 jax.ShapeDtypeStruct((nb, CONV_TAIL, d_br), F32)]
    if want_v:
        out_specs.append(pl.BlockSpec((tt, d_br), lambda b, i: (b * nt + i, 0)))
        out_shape.append(jax.ShapeDtypeStruct((nb * t, d_br), F32))
    return pl.pallas_call(
        functools.partial(_mixer_kernel, tt=tt, pos0=pos0, want_v=want_v),
        grid=(nb, nt),
        in_specs=in_specs,
        out_specs=out_specs,
        out_shape=out_shape,
        scratch_shapes=[pltpu.VMEM((POOL_HALO + tt, d_br), F32),
                        pltpu.VMEM((POOL_HALO + tt, d_br), F32),
                        pltpu.VMEM((GM_CHUNK, d_br), BF16)],
        compiler_params=_params(("arbitrary", "arbitrary")),
        name=f"mixer_t{t}",
    )(proj, proj, proj, proj, proj, proj, proj, proj, proj, proj, sp16, sc16,
      lw["pool_w"], lw["pool_scale"], lw["conv_w"], lw["gm_ln_g"], lw["gm_ln_b"], lw["gm_ws"], lw["gm_bst"],
      lw["w_branch"])


def _outproj_kernel(mp_ref, ms_ref, h_ref, w_ref, g_ref, b_ref, rwh_ref, rwl_ref, rb_ref,
                    h32_ref, hpk_ref, route_ref, gates_ref, counts_ref, carry_ref,
                    *, n_prompt_tiles, alpha, n_experts):
    i = pl.program_id(0)
    tm = h_ref.shape[0]

    @pl.when(i == 0)
    def _():
        carry_ref[...] = jnp.zeros(carry_ref.shape, F32)

    m = jnp.where(i < n_prompt_tiles, mp_ref[...], ms_ref[...])
    y = alpha * h_ref[...] + jnp.dot(m, w_ref[...], preferred_element_type=F32)
    h1 = _layer_norm(y, g_ref[...], b_ref[...])
    h32_ref[...] = h1
    hpk_ref[...] = _pack_halves(h1)

    hh = h1.astype(BF16)
    hl = (h1 - hh.astype(F32)).astype(BF16)
    logits = (jnp.dot(hh, rwh_ref[...], preferred_element_type=F32)
              + jnp.dot(hl, rwh_ref[...], preferred_element_type=F32)
              + jnp.dot(hh, rwl_ref[...], preferred_element_type=F32)) + rb_ref[...]
    lane = lax.broadcasted_iota(I32, (tm, LANES), 1)
    lane_f = lane.astype(F32)
    neg = jnp.float32(-jnp.inf)
    cur = jnp.where(lane < n_experts, logits, neg)
    vals, idxs = [], []
    for _ in range(TOP_K):
        mx = jnp.max(cur, axis=-1, keepdims=True)
        ik = jnp.min(jnp.where(cur == mx, lane_f, float(LANES)), axis=-1, keepdims=True).astype(I32)
        vals.append(mx)
        idxs.append(ik)
        cur = jnp.where(lane == ik, neg, cur)
    exps = [jnp.exp(v - vals[0]) for v in vals]
    denom = exps[0]
    for e in exps[1:]:
        denom = denom + e

    onehot = jnp.zeros((tm, LANES), F32)
    for ik in idxs:
        onehot = onehot + (lane == ik).astype(F32)
    below = (lax.broadcasted_iota(I32, (tm, tm), 0) > lax.broadcasted_iota(I32, (tm, tm), 1)).astype(BF16)
    prefix = jnp.dot(below, onehot.astype(BF16), preferred_element_type=F32) + carry_ref[...]
    route = jnp.zeros((tm, LANES), I32)
    gates = jnp.zeros((tm, LANES), F32)
    for k in range(TOP_K):
        rank = jnp.sum(jnp.where(lane == idxs[k], prefix, 0.0), axis=-1, keepdims=True).astype(I32)
        route = jnp.where(lane == k, idxs[k], route)
        route = jnp.where(lane == TOP_K + k, rank, route)
        gates = jnp.where(lane == k, exps[k] / denom, gates)
    route_ref[...] = route
    gates_ref[...] = gates
    carry_ref[...] = carry_ref[...] + jnp.sum(onehot, axis=0, keepdims=True)
    counts_ref[...] = jnp.broadcast_to(carry_ref[...], counts_ref.shape)


def _outproj(mp, ms, h32, lw, l, alpha, n_experts):
    n, d = h32.shape
    tp, ts = mp.shape[0] // ROW_TILE, ms.shape[0] // ROW_TILE
    return pl.pallas_call(
        functools.partial(_outproj_kernel, n_prompt_tiles=tp, alpha=alpha, n_experts=n_experts),
        grid=(tp + ts,),
        in_specs=[
            pl.BlockSpec((ROW_TILE, d), lambda i: (jnp.minimum(i, tp - 1), 0)),
            pl.BlockSpec((ROW_TILE, d), lambda i: (jnp.maximum(i - tp, 0), 0)),
            pl.BlockSpec((ROW_TILE, d), lambda i: (i, 0)),
            _resident((None, d, d), lambda i: (l, 0, 0)),
            _resident((None, 1, d), lambda i: (l, 0, 0)),
            _resident((None, 1, d), lambda i: (l, 0, 0)),
            _resident((None, d, LANES), lambda i: (l, 0, 0)),
            _resident((None, d, LANES), lambda i: (l, 0, 0)),
            _resident((None, 1, LANES), lambda i: (l, 0, 0)),
        ],
        out_specs=[pl.BlockSpec((ROW_TILE, d), lambda i: (i, 0)),
                   pl.BlockSpec((ROW_TILE, d // 2), lambda i: (i, 0)),
                   pl.BlockSpec((ROW_TILE, LANES), lambda i: (i, 0)),
                   pl.BlockSpec((ROW_TILE, LANES), lambda i: (i, 0)),
                   pl.BlockSpec((8, LANES), lambda i: (0, 0))],
        out_shape=[jax.ShapeDtypeStruct((n, d), F32),
                   jax.ShapeDtypeStruct((n, d // 2), I32),
                   jax.ShapeDtypeStruct((n, LANES), I32),
                   jax.ShapeDtypeStruct((n, LANES), F32),
                   jax.ShapeDtypeStruct((8, LANES), F32)],
        scratch_shapes=[pltpu.VMEM((1, LANES), F32)],
        compiler_params=_params(("arbitrary",)),
        name="outproj_router",
    )(mp, ms, h32, lw["w_out"], lw["ln1_g"], lw["ln1_b"], lw["router_wh"], lw["router_wl"], lw["router_b"])


def _dispatch_kernel(dest_ref, h_ref, xs_ref, sem):
    tm = h_ref.shape[0]

    def row_copy(r, k):
        return pltpu.make_async_copy(h_ref.at[pl.ds(r, 1)], xs_ref.at[pl.ds(dest_ref[r * TOP_K + k], 1)], sem)

    def start(r, c):
        for k in range(TOP_K):
            row_copy(r, k).start(priority=k % 2)
        return c

    def wait(r, c):
        for k in range(TOP_K):
            row_copy(r, k).wait()
        return c

    lax.fori_loop(0, tm, start, 0)
    lax.fori_loop(0, tm, wait, 0)


def _dispatch(hpk, dest_flat, n_rows):
    n, dh = hpk.shape
    return pl.pallas_call(
        _dispatch_kernel,
        grid=(n // ROW_TILE,),
        in_specs=[
            pl.BlockSpec((ROW_TILE * TOP_K,), lambda i: (i,), memory_space=pltpu.SMEM),
            pl.BlockSpec((ROW_TILE, dh), lambda i: (i, 0)),
        ],
        out_specs=pl.BlockSpec(memory_space=pl.ANY),
        out_shape=jax.ShapeDtypeStruct((n_rows, dh), I32),
        scratch_shapes=[pltpu.SemaphoreType.DMA(())],
        compiler_params=_params(("arbitrary",)),
        name="dispatch",
    )(dest_flat, hpk)


def _experts_kernel(be_ref, br_ref, nv_ref, x_ref, wg_ref, wl_ref, bg_ref, bl_ref, wd_ref, bd_ref, o_ref,
                    x16_ref, acc_ref):
    del be_ref, br_ref
    b = pl.program_id(0)
    f = pl.program_id(1)
    nf = pl.num_programs(1)
    n_valid = nv_ref[b]
    tm, half = x_ref.shape

    @pl.when(n_valid > 0)
    def _():
        @pl.when(f == 0)
        def _():
            keep = lax.broadcasted_iota(I32, (tm, 1), 0) < n_valid
            lo, hi = _unpack_halves(x_ref[...])
            x16_ref[:, :half] = jnp.where(keep, lo, 0.0).astype(BF16)
            x16_ref[:, half:] = jnp.where(keep, hi, 0.0).astype(BF16)
            acc_ref[...] = jnp.broadcast_to(bd_ref[...], acc_ref.shape)

        x = x16_ref[...]
        hg = jnp.dot(x, wg_ref[...].astype(BF16), preferred_element_type=F32) + bg_ref[...]
        hl = jnp.dot(x, wl_ref[...].astype(BF16), preferred_element_type=F32) + bl_ref[...]
        glu = jnp.minimum(hg, SWIGLU_LIMIT)
        lin = jnp.clip(hl, -SWIGLU_LIMIT, SWIGLU_LIMIT)
        a = glu * jax.nn.sigmoid(SWIGLU_ALPHA * glu) * (lin + 1.0)
        acc_ref[...] += jnp.dot(a.astype(BF16), wd_ref[...].astype(BF16), preferred_element_type=F32)

        @pl.when(f == nf - 1)
        def _():
            o_ref[...] = _pack_halves(acc_ref[...])

    @pl.when(jnp.logical_and(n_valid == 0, f == 0))
    def _():
        o_ref[...] = jnp.zeros(o_ref.shape, I32)


def _experts(xs, block_e, block_row, block_valid, lw, l):
    rows, dh = xs.shape
    d = 2 * dh
    d_ff = lw["exp_w_up"].shape[3] // 2
    nf = d_ff // MOE_FF_TILE
    n_blocks = rows // MOE_BLOCK
    tf = MOE_FF_TILE
    grid_spec = pltpu.PrefetchScalarGridSpec(
        num_scalar_prefetch=3,
        grid=(n_blocks, nf),
        in_specs=[
            pl.BlockSpec((MOE_BLOCK, dh), lambda b, f, be, br, nv: (br[b], 0)),
            pl.BlockSpec((None, None, d, tf), lambda b, f, be, br, nv: (l, be[b], 0, f)),
            pl.BlockSpec((None, None, d, tf), lambda b, f, be, br, nv: (l, be[b], 0, nf + f)),
            pl.BlockSpec((None, None, 1, tf), lambda b, f, be, br, nv: (l, be[b], 0, f)),
            pl.BlockSpec((None, None, 1, tf), lambda b, f, be, br, nv: (l, be[b], 0, nf + f)),
            pl.BlockSpec((None, None, tf, d), lambda b, f, be, br, nv: (l, be[b], f, 0)),
            pl.BlockSpec((None, None, 1, d), lambda b, f, be, br, nv: (l, be[b], 0, 0)),
        ],
        out_specs=pl.BlockSpec((MOE_BLOCK, dh), lambda b, f, be, br, nv: (b, 0)),
        scratch_shapes=[pltpu.VMEM((MOE_BLOCK, d), BF16), pltpu.VMEM((MOE_BLOCK, d), F32)],
    )
    return pl.pallas_call(
        _experts_kernel,
        grid_spec=grid_spec,
        out_shape=jax.ShapeDtypeStruct((rows, dh), I32),
        compiler_params=_params(("arbitrary", "arbitrary")),
        name="experts",
    )(block_e, block_row, block_valid, xs, lw["exp_w_up"], lw["exp_w_up"], lw["exp_b_up"], lw["exp_b_up"],
      lw["exp_w_down"], lw["exp_b_down"])


def _combine_kernel(dest_ref, ys_ref, gates_ref, h_ref, g_ref, b_ref, h32_ref, h16_ref, buf_ref, sem, *, alpha):
    tm = h_ref.shape[0]

    def row_copy(r, k):
        return pltpu.make_async_copy(ys_ref.at[pl.ds(dest_ref[r * TOP_K + k], 1)],
                                     buf_ref.at[k, pl.ds(r, 1)], sem)

    def start(r, c):
        for k in range(TOP_K):
            row_copy(r, k).start(priority=k % 2)
        return c

    def wait(r, c):
        for k in range(TOP_K):
            row_copy(r, k).wait()
        return c

    lax.fori_loop(0, tm, start, 0)
    lax.fori_loop(0, tm, wait, 0)
    gates = gates_ref[...]
    moe_lo = moe_hi = None
    for k in range(TOP_K):
        lo, hi = _unpack_halves(buf_ref[k])
        gk = gates[:, k:k + 1]
        moe_lo = lo * gk if moe_lo is None else moe_lo + lo * gk
        moe_hi = hi * gk if moe_hi is None else moe_hi + hi * gk
    moe = jnp.concatenate([moe_lo, moe_hi], axis=-1)
    y = _layer_norm(alpha * h_ref[...] + moe, g_ref[...], b_ref[...])
    h32_ref[...] = y
    h16_ref[...] = y.astype(BF16)


def _combine(ys, dest_flat, gates, h32, g, b, l, alpha):
    n, d = h32.shape
    return pl.pallas_call(
        functools.partial(_combine_kernel, alpha=alpha),
        grid=(n // ROW_TILE,),
        in_specs=[
            pl.BlockSpec((ROW_TILE * TOP_K,), lambda i: (i,), memory_space=pltpu.SMEM),
            pl.BlockSpec(memory_space=pl.ANY),
            pl.BlockSpec((ROW_TILE, LANES), lambda i: (i, 0)),
            pl.BlockSpec((ROW_TILE, d), lambda i: (i, 0)),
            pl.BlockSpec((None, 1, d), lambda i: (l, 0, 0)),
            pl.BlockSpec((None, 1, d), lambda i: (l, 0, 0)),
        ],
        out_specs=[pl.BlockSpec((ROW_TILE, d), lambda i: (i, 0)),
                   pl.BlockSpec((ROW_TILE, d), lambda i: (i, 0))],
        out_shape=[jax.ShapeDtypeStruct((n, d), F32), jax.ShapeDtypeStruct((n, d), BF16)],
        scratch_shapes=[pltpu.VMEM((TOP_K, ROW_TILE, d // 2), I32), pltpu.SemaphoreType.DMA(())],
        compiler_params=_params(("arbitrary",)),
        name="combine",
    )(dest_flat, ys, gates, h32, g, b)


def _routing_tables(route, counts_f, n_experts, n_blocks):
    idx = route[:, :TOP_K]
    rank = route[:, TOP_K:2 * TOP_K]
    counts = counts_f[0, :n_experts].astype(I32)
    padded = (counts + MOE_BLOCK - 1) // MOE_BLOCK * MOE_BLOCK
    pad_end = jnp.cumsum(padded)
    pad_start = pad_end - padded
    dest = (pad_start[idx] + rank).reshape(-1).astype(I32)
    n_used = pad_end[-1] // MOE_BLOCK
    blk = jnp.minimum(jnp.arange(n_blocks, dtype=I32), n_used - 1)
    block_e = jnp.sum((pad_end[None, :] <= (blk * MOE_BLOCK)[:, None]).astype(I32), axis=1)
    block_e = jnp.minimum(block_e, n_experts - 1)
    in_use = jnp.arange(n_blocks, dtype=I32) < n_used
    valid = jnp.clip(counts[block_e] - (blk * MOE_BLOCK - pad_start[block_e]), 0, MOE_BLOCK)
    valid = jnp.where(in_use, valid, 0).astype(I32)
    return dest, block_e.astype(I32), blk.astype(I32), valid


def _pad_state(state, rows):
    return jnp.pad(state, ((0, 0), (0, 0), (rows - state.shape[2], 0), (0, 0)))


def kernel(x_prompt, x_sample, state_pool, state_conv, ln_in_g, ln_in_b, w_in, b_in, pool_w, pool_scale, conv_w,
           gm_ln_g, gm_ln_b, gm_ws, gm_bs, w_branch, w_out, ln1_g, ln1_b, router_w, router_b, exp_w_up,
           exp_b_up, exp_w_down, exp_b_down, ln2_g, ln2_b):
    bp, seq, d = x_prompt.shape
    bs, dseq, _ = x_sample.shape
    depth = w_in.shape[0]
    d_br = pool_scale.shape[1]
    n_experts = router_w.shape[2]
    alpha = float((2 * depth) ** 0.25)
    n_p, n_s = bp * seq, bs * dseq
    n = n_p + n_s
    n_blocks = (n * TOP_K) // MOE_BLOCK + n_experts
    pool_state = POOL_WINDOWS[-1] - 1

    rw = jnp.pad(router_w, ((0, 0), (0, 0), (0, LANES - n_experts)))
    rwh = rw.astype(BF16)
    lw = dict(
        pool_w=pool_w.astype(BF16), pool_scale=pool_scale.reshape(depth, 1, d_br), conv_w=conv_w,
        gm_ln_g=gm_ln_g.reshape(depth, 1, d_br), gm_ln_b=gm_ln_b.reshape(depth, 1, d_br),
        gm_ws=gm_ws.astype(BF16), gm_bst=jnp.swapaxes(gm_bs, 1, 2), w_branch=w_branch.astype(BF16),
        w_out=w_out.astype(BF16), ln1_g=ln1_g.reshape(depth, 1, d), ln1_b=ln1_b.reshape(depth, 1, d),
        router_wh=rwh, router_wl=(rw - rwh.astype(F32)).astype(BF16),
        router_b=jnp.pad(router_b, ((0, 0), (0, LANES - n_experts))).reshape(depth, 1, LANES),
        exp_w_up=exp_w_up, exp_b_up=exp_b_up.reshape(depth, n_experts, 1, -1),
        exp_w_down=exp_w_down, exp_b_down=exp_b_down.reshape(depth, n_experts, 1, d),
    )
    w_in16 = w_in.astype(BF16)
    b_in3 = b_in.reshape(depth, 1, -1)
    ln2_g3, ln2_b3 = ln2_g.reshape(depth, 1, d), ln2_b.reshape(depth, 1, d)
    sp16 = _pad_state(state_pool, POOL_HALO)
    sc16 = _pad_state(state_conv, POOL_HALO)
    zero_state = jnp.zeros((bp, POOL_HALO, d_br), F32)

    h32, h16 = _ln_in(x_prompt.reshape(n_p, d), x_sample.reshape(n_s, d), ln_in_g, ln_in_b)
    new_pool_p, new_conv_p, new_pool_s, new_conv_s, v_rows = [], [], [], [], []
    for l in range(depth):
        proj = _proj(h16, w_in16, b_in3, l)
        mp, ptail_p, ctail_p = _mixer(proj, 0, bp, seq, MIX_TILE, 0, zero_state, zero_state, lw, l, False)
        ms, ptail_s, ctail_s, vn = _mixer(proj, n_p, bs, dseq, dseq, PAST_LEN, sp16[l], sc16[l], lw, l, True)
        h1, hpk, route, gates, counts = _outproj(mp, ms, h32, lw, l, alpha, n_experts)
        dest, block_e, block_row, block_valid = _routing_tables(route, counts, n_experts, n_blocks)
        xs = _dispatch(hpk, dest, n_blocks * MOE_BLOCK)
        ys = _experts(xs, block_e, block_row, block_valid, lw, l)
        h32, h16 = _combine(ys, dest, gates, h1, ln2_g3, ln2_b3, l, alpha)
        new_pool_p.append(ptail_p[:, POOL_HALO - pool_state:, :])
        new_conv_p.append(ctail_p[:, CONV_TAIL - (CONV_WIDTH - 1):, :])
        new_pool_s.append(ptail_s[:, POOL_HALO - pool_state:, :])
        new_conv_s.append(ctail_s[:, CONV_TAIL - (CONV_WIDTH - 1):, :])
        v_rows.append(vn.reshape(bs, dseq, d_br))
    y_prompt = h32[:n_p].reshape(bp, seq, d)
    y_sample = h32[n_p:].reshape(bs, dseq, d)
    return (y_prompt, y_sample, jnp.stack(new_pool_p), jnp.stack(new_conv_p), jnp.stack(new_pool_s),
            jnp.stack(new_conv_s), jnp.stack(v_rows))
```

```python
import functools

import jax
import jax.numpy as jnp
from jax import lax
from jax.experimental import pallas as pl
from jax.experimental.pallas import tpu as pltpu

F32 = jnp.float32
BF16 = jnp.bfloat16
I32 = jnp.int32

POOL_WINDOWS = (2, 4, 8, 16)
POOL_HALO = 16
CONV_WIDTH = 3
CONV_TAIL = 8
GM_CHUNK = 128
GM_HEADS = 8
N_BRANCH = 3
TOP_K = 4
SWIGLU_LIMIT = 7.0
SWIGLU_ALPHA = 1.702
LN_EPS = 1e-5
PAST_LEN = 4096
LANES = 128

ROW_TILE = 256
MIX_TILE = 256
MOE_BLOCK = 1024
MOE_FF_TILE = 512
VMEM_LIMIT = 60 * 1024 * 1024
MOE_SUB = 256
TOK_ROWS = 8
HIGH_HALF = -65536


def _params(sem, vmem=VMEM_LIMIT):
    return pltpu.CompilerParams(dimension_semantics=sem, vmem_limit_bytes=vmem)


def _layer_norm(x, g, b):
    mu = jnp.mean(x, axis=-1, keepdims=True)
    xc = x - mu
    var = jnp.mean(xc * xc, axis=-1, keepdims=True)
    return xc * lax.rsqrt(var + LN_EPS) * g + b


def _resident(shape, index_map):
    return pl.BlockSpec(shape, index_map, pipeline_mode=pl.Buffered(1))


def _pack_halves(y):
    half = y.shape[1] // 2
    bits = pltpu.bitcast(y.astype(BF16).astype(F32), I32)
    return lax.shift_right_logical(bits[:, :half], 16) | bits[:, half:]


def _unpack_halves(w):
    return pltpu.bitcast(w << 16, F32), pltpu.bitcast(w & HIGH_HALF, F32)


def _store_token_tiles(ref, packed, lead=()):
    m = packed.shape[0]
    for c in range(TOK_ROWS):
        ref[lead + (pl.ds(c, m, stride=TOK_ROWS), slice(None))] = packed[:, c * LANES:(c + 1) * LANES]


def _load_token_tiles(ref, m, c, lead=()):
    return ref[lead + (pl.ds(c, m, stride=TOK_ROWS), slice(None))]


def _ln_in_kernel(xp_ref, xs_ref, g_ref, b_ref, h32_ref, h16_ref, *, n_prompt_tiles):
    i = pl.program_id(0)
    x = jnp.where(i < n_prompt_tiles, xp_ref[...], xs_ref[...])
    y = _layer_norm(x, g_ref[...], b_ref[...])
    h32_ref[...] = y
    h16_ref[...] = y.astype(BF16)


def _ln_in(xp, xs, g, b):
    n_p, d = xp.shape
    n_s = xs.shape[0]
    tp, ts = n_p // ROW_TILE, n_s // ROW_TILE
    n = n_p + n_s
    return pl.pallas_call(
        functools.partial(_ln_in_kernel, n_prompt_tiles=tp),
        grid=(tp + ts,),
        in_specs=[
            pl.BlockSpec((ROW_TILE, d), lambda i: (jnp.minimum(i, tp - 1), 0)),
            pl.BlockSpec((ROW_TILE, d), lambda i: (jnp.maximum(i - tp, 0), 0)),
            pl.BlockSpec((1, d), lambda i: (0, 0)),
            pl.BlockSpec((1, d), lambda i: (0, 0)),
        ],
        out_specs=[pl.BlockSpec((ROW_TILE, d), lambda i: (i, 0)),
                   pl.BlockSpec((ROW_TILE, d), lambda i: (i, 0))],
        out_shape=[jax.ShapeDtypeStruct((n, d), F32), jax.ShapeDtypeStruct((n, d), BF16)],
        compiler_params=_params(("arbitrary",)),
        name="ln_in",
    )(xp, xs, g.reshape(1, d), b.reshape(1, d))


def _proj_kernel(x_ref, w_ref, b_ref, o_ref):
    o_ref[...] = (jnp.dot(x_ref[...], w_ref[...], preferred_element_type=F32) + b_ref[...]).astype(o_ref.dtype)


def _pick_tile(n, candidates):
    for c in candidates:
        if n % c == 0:
            return c
    raise ValueError(f"no tile in {candidates} divides {n}")


def _proj(h16, w16, b, l):
    n, d = h16.shape
    d_in = w16.shape[2]
    bm = _pick_tile(n, (1280, 1024, 512, 256))
    bn = 1024
    return pl.pallas_call(
        _proj_kernel,
        grid=(n // bm, d_in // bn),
        in_specs=[
            pl.BlockSpec((bm, d), lambda i, j: (i, 0)),
            pl.BlockSpec((None, d, bn), lambda i, j: (l, 0, j)),
            pl.BlockSpec((None, 1, bn), lambda i, j: (l, 0, j)),
        ],
        out_specs=pl.BlockSpec((bm, bn), lambda i, j: (i, j)),
        out_shape=jax.ShapeDtypeStruct((n, d_in), BF16),
        compiler_params=_params(("arbitrary", "arbitrary")),
        name="proj",
    )(h16, w16, b)


def _mixer_kernel(gate_ref, p_ref, xc_ref, bg_ref, cg_ref, u_ref, v_ref, hp_ref, hxc_ref, hcg_ref,
                  sp_ref, sc_ref, poolw_ref, pscale_ref, convw_ref, lng_ref, lnb_ref, ws_ref, bst_ref,
                  wbr_ref, *rest, tt, pos0, want_v):
    if want_v:
        merged_ref, ptail_ref, ctail_ref, vn_ref, extp_ref, extz_ref, vpad_ref = rest
    else:
        merged_ref, ptail_ref, ctail_ref, extp_ref, extz_ref, vpad_ref = rest
        vn_ref = None
    i = pl.program_id(1)
    first = i == 0
    d_br = p_ref.shape[1]
    d_model = merged_ref.shape[1]
    gdim = d_br // len(POOL_WINDOWS)

    p = p_ref[...].astype(F32)
    extp_ref[0:POOL_HALO, :] = jnp.where(first, sp_ref[0], hp_ref[...].astype(F32))
    extp_ref[POOL_HALO:POOL_HALO + tt, :] = p
    pos = lax.broadcasted_iota(I32, (tt, 1), 0) + (i * tt + pos0)
    a_parts = []
    for g, win in enumerate(POOL_WINDOWS):
        cols = slice(g * gdim, (g + 1) * gdim)
        s = p[:, cols]
        for j in range(1, win):
            s = s + extp_ref[POOL_HALO - j:POOL_HALO - j + tt, cols]
        cnt = jnp.minimum(pos + 1, win).astype(F32)
        mixed = s / cnt - p[:, cols]
        a_parts.append(jnp.dot(mixed.astype(BF16), poolw_ref[g], preferred_element_type=F32))
    a_out = jnp.concatenate(a_parts, axis=-1) * pscale_ref[...]

    z = cg_ref[...].astype(F32) * xc_ref[...].astype(F32)
    extz_ref[0:POOL_HALO, :] = jnp.where(first, sc_ref[0], hcg_ref[...].astype(F32) * hxc_ref[...].astype(F32))
    extz_ref[POOL_HALO:POOL_HALO + tt, :] = z
    y = z * convw_ref[CONV_WIDTH - 1:CONV_WIDTH, :]
    for k in range(CONV_WIDTH - 1):
        off = POOL_HALO - (CONV_WIDTH - 1) + k
        y = y + extz_ref[off:off + tt, :] * convw_ref[k:k + 1, :]
    c_out = bg_ref[...].astype(F32) * y

    vn = _layer_norm(v_ref[...].astype(F32), lng_ref[...], lnb_ref[...])
    if want_v:
        vn_ref[...] = vn
    hd = d_br // GM_HEADS
    tri = (lax.broadcasted_iota(I32, (GM_CHUNK, GM_CHUNK), 0)
           >= lax.broadcasted_iota(I32, (GM_CHUNK, GM_CHUNK), 1))
    wms = [jnp.where(tri, ws_ref[h], jnp.zeros((), BF16)) for h in range(GM_HEADS)]
    s_chunks = []
    for c in range(max(tt // GM_CHUNK, 1)):
        if tt < GM_CHUNK:
            vpad_ref[...] = jnp.zeros(vpad_ref.shape, BF16)
            vpad_ref[0:tt, :] = vn.astype(BF16)
            vchunk = vpad_ref[...]
        else:
            vchunk = vn[c * GM_CHUNK:(c + 1) * GM_CHUNK, :].astype(BF16)
        heads = []
        for h in range(GM_HEADS):
            sh = jnp.dot(wms[h], vchunk[:, h * hd:(h + 1) * hd], preferred_element_type=F32)
            heads.append(sh + bst_ref[:, h:h + 1])
        s_chunks.append(jnp.concatenate(heads, axis=-1)[:min(tt, GM_CHUNK), :])
    s_all = s_chunks[0] if len(s_chunks) == 1 else jnp.concatenate(s_chunks, axis=0)
    g_out = u_ref[...].astype(F32) * s_all

    merged = None
    for n, br in enumerate((a_out, c_out, g_out)):
        t = jnp.dot(br.astype(BF16), wbr_ref[n], preferred_element_type=F32)
        t = jax.nn.sigmoid(gate_ref[:, n * d_model:(n + 1) * d_model].astype(F32)) * t
        merged = t if merged is None else merged + t
    merged_ref[...] = merged.astype(merged_ref.dtype)
    ptail_ref[0] = p[tt - POOL_HALO:tt, :]
    ctail_ref[0] = z[tt - CONV_TAIL:tt, :]


def _mixer(proj, row_off, nb, t, tt, pos0, sp16, sc16, lw, l, want_v):
    d_br = sp16.shape[2]
    d_model = lw["w_branch"].shape[3]
    gate_w = N_BRANCH * d_model
    col0 = gate_w // d_br
    nt = t // tt
    base = row_off // tt

    def row(b, i):
        return base + b * nt + i

    def halo(b, i):
        return jnp.maximum((row_off + b * t + i * tt) // POOL_HALO - 1, 0)

    in_specs = [pl.BlockSpec((tt, gate_w), lambda b, i: (row(b, i), 0))]
    in_specs += [pl.BlockSpec((tt, d_br), functools.partial(lambda b, i, c: (row(b, i), col0 + c), c=c))
                 for c in range(6)]
    in_specs += [pl.BlockSpec((POOL_HALO, d_br), functools.partial(lambda b, i, c: (halo(b, i), col0 + c), c=c))
                 for c in (0, 1, 3)]
    in_specs += [pl.BlockSpec((1, POOL_HALO, d_br), lambda b, i: (b, 0, 0)),
                 pl.BlockSpec((1, POOL_HALO, d_br), lambda b, i: (b, 0, 0))]
    gdim = d_br // len(POOL_WINDOWS)
    in_specs += [
        _resident((None, len(POOL_WINDOWS), gdim, gdim), lambda b, i: (l, 0, 0, 0)),
        _resident((None, 1, d_br), lambda b, i: (l, 0, 0)),
        _resident((None, CONV_WIDTH, d_br), lambda b, i: (l, 0, 0)),
        _resident((None, 1, d_br), lambda b, i: (l, 0, 0)),
        _resident((None, 1, d_br), lambda b, i: (l, 0, 0)),
        _resident((None, GM_HEADS, GM_CHUNK, GM_CHUNK), lambda b, i: (l, 0, 0, 0)),
        _resident((None, GM_CHUNK, GM_HEADS), lambda b, i: (l, 0, 0)),
        _resident((None, N_BRANCH, d_br, d_model), lambda b, i: (l, 0, 0, 0)),
    ]
    out_specs = [pl.BlockSpec((tt, d_model), lambda b, i: (b * nt + i, 0)),
                 pl.BlockSpec((1, POOL_HALO, d_br), lambda b, i: (b, 0, 0)),
                 pl.BlockSpec((1, CONV_TAIL, d_br), lambda b, i: (b, 0, 0))]
    out_shape = [jax.ShapeDtypeStruct((nb * t, d_model), BF16),
                 jax.ShapeDtypeStruct((nb, POOL_HALO, d_br), F32),
                 jax.ShapeDtypeStruct((nb, CONV_TAIL, d_br), F32)]
    if want_v:
        out_specs.append(pl.BlockSpec((tt, d_br), lambda b, i: (b * nt + i, 0)))
        out_shape.append(jax.ShapeDtypeStruct((nb * t, d_br), F32))
    return pl.pallas_call(
        functools.partial(_mixer_kernel, tt=tt, pos0=pos0, want_v=want_v),
        grid=(nb, nt),
        in_specs=in_specs,
        out_specs=out_specs,
        out_shape=out_shape,
        scratch_shapes=[pltpu.VMEM((POOL_HALO + tt, d_br), F32),
                        pltpu.VMEM((POOL_HALO + tt, d_br), F32),
                        pltpu.VMEM((GM_CHUNK, d_br), BF16)],
        compiler_params=_params(("arbitrary", "arbitrary")),
        name=f"mixer_t{t}",
    )(proj, proj, proj, proj, proj, proj, proj, proj, proj, proj, sp16, sc16,
      lw["pool_w"], lw["pool_scale"], lw["conv_w"], lw["gm_ln_g"], lw["gm_ln_b"], lw["gm_ws"], lw["gm_bst"],
      lw["w_branch"])


def _outproj_kernel(mp_ref, ms_ref, h_ref, w_ref, g_ref, b_ref, rwh_ref, rwl_ref, rb_ref,
                    h32_ref, hpk_ref, route_ref, gates_ref, counts_ref, carry_ref,
                    *, n_prompt_tiles, alpha, n_experts):
    i = pl.program_id(0)
    tm = h_ref.shape[0]

    @pl.when(i == 0)
    def _():
        carry_ref[...] = jnp.zeros(carry_ref.shape, F32)

    m = jnp.where(i < n_prompt_tiles, mp_ref[...], ms_ref[...])
    y = alpha * h_ref[...] + jnp.dot(m, w_ref[...], preferred_element_type=F32)
    h1 = _layer_norm(y, g_ref[...], b_ref[...])
    h32_ref[...] = h1
    _store_token_tiles(hpk_ref, _pack_halves(h1))

    hh = h1.astype(BF16)
    hl = (h1 - hh.astype(F32)).astype(BF16)
    logits = (jnp.dot(hh, rwh_ref[...], preferred_element_type=F32)
              + jnp.dot(hl, rwh_ref[...], preferred_element_type=F32)
              + jnp.dot(hh, rwl_ref[...], preferred_element_type=F32)) + rb_ref[...]
    lane = lax.broadcasted_iota(I32, (tm, LANES), 1)
    lane_f = lane.astype(F32)
    neg = jnp.float32(-jnp.inf)
    cur = jnp.where(lane < n_experts, logits, neg)
    vals, idxs = [], []
    for _ in range(TOP_K):
        mx = jnp.max(cur, axis=-1, keepdims=True)
        ik = jnp.min(jnp.where(cur == mx, lane_f, float(LANES)), axis=-1, keepdims=True).astype(I32)
        vals.append(mx)
        idxs.append(ik)
        cur = jnp.where(lane == ik, neg, cur)
    exps = [jnp.exp(v - vals[0]) for v in vals]
    denom = exps[0]
    for e in exps[1:]:
        denom = denom + e

    onehot = jnp.zeros((tm, LANES), F32)
    for ik in idxs:
        onehot = onehot + (lane == ik).astype(F32)
    below = (lax.broadcasted_iota(I32, (tm, tm), 0) > lax.broadcasted_iota(I32, (tm, tm), 1)).astype(BF16)
    prefix = jnp.dot(below, onehot.astype(BF16), preferred_element_type=F32) + carry_ref[...]
    route = jnp.zeros((tm, LANES), I32)
    gates = jnp.zeros((tm, LANES), F32)
    for k in range(TOP_K):
        rank = jnp.sum(jnp.where(lane == idxs[k], prefix, 0.0), axis=-1, keepdims=True).astype(I32)
        route = jnp.where(lane == k, idxs[k], route)
        route = jnp.where(lane == TOP_K + k, rank, route)
        gates = jnp.where(lane == k, exps[k] / denom, gates)
    route_ref[...] = route
    gates_ref[...] = gates
    carry_ref[...] = carry_ref[...] + jnp.sum(onehot, axis=0, keepdims=True)
    counts_ref[...] = jnp.broadcast_to(carry_ref[...], counts_ref.shape)


def _outproj(mp, ms, h32, lw, l, alpha, n_experts):
    n, d = h32.shape
    tp, ts = mp.shape[0] // ROW_TILE, ms.shape[0] // ROW_TILE
    return pl.pallas_call(
        functools.partial(_outproj_kernel, n_prompt_tiles=tp, alpha=alpha, n_experts=n_experts),
        grid=(tp + ts,),
        in_specs=[
            pl.BlockSpec((ROW_TILE, d), lambda i: (jnp.minimum(i, tp - 1), 0)),
            pl.BlockSpec((ROW_TILE, d), lambda i: (jnp.maximum(i - tp, 0), 0)),
            pl.BlockSpec((ROW_TILE, d), lambda i: (i, 0)),
            _resident((None, d, d), lambda i: (l, 0, 0)),
            _resident((None, 1, d), lambda i: (l, 0, 0)),
            _resident((None, 1, d), lambda i: (l, 0, 0)),
            _resident((None, d, LANES), lambda i: (l, 0, 0)),
            _resident((None, d, LANES), lambda i: (l, 0, 0)),
            _resident((None, 1, LANES), lambda i: (l, 0, 0)),
        ],
        out_specs=[pl.BlockSpec((ROW_TILE, d), lambda i: (i, 0)),
                   pl.BlockSpec((ROW_TILE * TOK_ROWS, LANES), lambda i: (i, 0)),
                   pl.BlockSpec((ROW_TILE, LANES), lambda i: (i, 0)),
                   pl.BlockSpec((ROW_TILE, LANES), lambda i: (i, 0)),
                   pl.BlockSpec((8, LANES), lambda i: (0, 0))],
        out_shape=[jax.ShapeDtypeStruct((n, d), F32),
                   jax.ShapeDtypeStruct((n * TOK_ROWS, LANES), I32),
                   jax.ShapeDtypeStruct((n, LANES), I32),
                   jax.ShapeDtypeStruct((n, LANES), F32),
                   jax.ShapeDtypeStruct((8, LANES), F32)],
        scratch_shapes=[pltpu.VMEM((1, LANES), F32)],
        compiler_params=_params(("arbitrary",)),
        name="outproj_router",
    )(mp, ms, h32, lw["w_out"], lw["ln1_g"], lw["ln1_b"], lw["router_wh"], lw["router_wl"], lw["router_b"])


def _token_tile(ref, row):
    return ref.at[pl.ds(pl.multiple_of(row * TOK_ROWS, TOK_ROWS), TOK_ROWS)]


def _dispatch_kernel(zlo_ref, zhi_ref, dest_ref, h_ref, xs_ref, zero_ref, sem, zsem):
    tm = h_ref.shape[0] // TOK_ROWS

    @pl.when(pl.program_id(0) == 0)
    def _():
        zero_ref[...] = jnp.zeros(zero_ref.shape, I32)

        def zero_copy(row):
            return pltpu.make_async_copy(zero_ref, _token_tile(xs_ref, row), zsem)

        def each_range(g, c):
            lax.fori_loop(zlo_ref[g], zhi_ref[g], lambda row, cc: (zero_copy(row).start(), cc)[1], 0)
            lax.fori_loop(zlo_ref[g], zhi_ref[g], lambda row, cc: (zero_copy(row).wait(), cc)[1], 0)
            return c

        lax.fori_loop(0, zlo_ref.shape[0], each_range, 0)

    def row_copy(r, k):
        return pltpu.make_async_copy(_token_tile(h_ref, r), _token_tile(xs_ref, dest_ref[r * TOP_K + k]), sem)

    def start(r, c):
        for k in range(TOP_K):
            row_copy(r, k).start(priority=k % 2)
        return c

    def wait(r, c):
        for k in range(TOP_K):
            row_copy(r, k).wait()
        return c

    lax.fori_loop(0, tm, start, 0, unroll=4)
    lax.fori_loop(0, tm, wait, 0, unroll=4)


def _dispatch(hpk, dest_flat, zero_lo, zero_hi, n_rows):
    n = hpk.shape[0] // TOK_ROWS
    grid_spec = pltpu.PrefetchScalarGridSpec(
        num_scalar_prefetch=2,
        grid=(n // ROW_TILE,),
        in_specs=[
            pl.BlockSpec((ROW_TILE * TOP_K,), lambda i, zl, zh: (i,), memory_space=pltpu.SMEM),
            pl.BlockSpec((ROW_TILE * TOK_ROWS, LANES), lambda i, zl, zh: (i, 0)),
        ],
        out_specs=pl.BlockSpec(memory_space=pl.ANY),
        scratch_shapes=[pltpu.VMEM((TOK_ROWS, LANES), I32), pltpu.SemaphoreType.DMA(()),
                        pltpu.SemaphoreType.DMA(())],
    )
    return pl.pallas_call(
        _dispatch_kernel,
        grid_spec=grid_spec,
        out_shape=jax.ShapeDtypeStruct((n_rows * TOK_ROWS, LANES), I32),
        compiler_params=_params(("arbitrary",)),
        name="dispatch",
    )(zero_lo, zero_hi, dest_flat, hpk)


def _experts_kernel(be_ref, br_ref, nv_ref, x_ref, wg_ref, wl_ref, bg_ref, bl_ref, wd_ref, bd_ref, o_ref,
                    x16_ref, acc_ref):
    del be_ref, br_ref
    b = pl.program_id(0)
    f = pl.program_id(1)
    nf = pl.num_programs(1)
    n_valid = nv_ref[b]
    tm, d = x16_ref.shape
    half = d // 2

    @pl.when(n_valid > 0)
    def _():
        @pl.when(f == 0)
        def _():
            for c in range(TOK_ROWS):
                lo, hi = _unpack_halves(_load_token_tiles(x_ref, tm, c))
                x16_ref[:, c * LANES:(c + 1) * LANES] = lo.astype(BF16)
                x16_ref[:, half + c * LANES:half + (c + 1) * LANES] = hi.astype(BF16)
            acc_ref[...] = jnp.broadcast_to(bd_ref[...], acc_ref.shape)

        def run(groups):
            wg16 = wg_ref[...].astype(BF16)
            wl16 = wl_ref[...].astype(BF16)
            wd16 = wd_ref[...].astype(BF16)
            row0 = 0
            for cm in groups:
                rows = slice(row0, row0 + cm)
                row0 += cm
                x = x16_ref[rows, :]
                hg = jnp.dot(x, wg16, preferred_element_type=F32) + bg_ref[...]
                hl = jnp.dot(x, wl16, preferred_element_type=F32) + bl_ref[...]
                glu = jnp.minimum(hg, SWIGLU_LIMIT)
                lin = jnp.clip(hl, -SWIGLU_LIMIT, SWIGLU_LIMIT)
                a = glu * jax.nn.sigmoid(SWIGLU_ALPHA * glu) * (lin + 1.0)
                acc_ref[rows, :] += jnp.dot(a.astype(BF16), wd16, preferred_element_type=F32)

        n_sub = (n_valid + (MOE_SUB - 1)) // MOE_SUB
        for q in range(1, tm // MOE_SUB + 1):
            groups = ((q - 1) * MOE_SUB, MOE_SUB) if q > 1 else (MOE_SUB,)
            pl.when(n_sub == q)(functools.partial(run, groups))

        @pl.when(f == nf - 1)
        def _():
            _store_token_tiles(o_ref, _pack_halves(acc_ref[...]))

    @pl.when(jnp.logical_and(n_valid == 0, f == 0))
    def _():
        o_ref[...] = jnp.zeros(o_ref.shape, I32)


def _experts(xs, block_e, block_row, block_valid, lw, l):
    rows = xs.shape[0] // TOK_ROWS
    d = lw["exp_w_down"].shape[3]
    d_ff = lw["exp_w_up"].shape[3] // 2
    nf = d_ff // MOE_FF_TILE
    n_blocks = rows // MOE_BLOCK
    tf = MOE_FF_TILE
    grid_spec = pltpu.PrefetchScalarGridSpec(
        num_scalar_prefetch=3,
        grid=(n_blocks, nf),
        in_specs=[
            pl.BlockSpec((MOE_BLOCK * TOK_ROWS, LANES), lambda b, f, be, br, nv: (br[b], 0)),
            pl.BlockSpec((None, None, d, tf), lambda b, f, be, br, nv: (l, be[b], 0, f)),
            pl.BlockSpec((None, None, d, tf), lambda b, f, be, br, nv: (l, be[b], 0, nf + f)),
            pl.BlockSpec((None, None, 1, tf), lambda b, f, be, br, nv: (l, be[b], 0, f)),
            pl.BlockSpec((None, None, 1, tf), lambda b, f, be, br, nv: (l, be[b], 0, nf + f)),
            pl.BlockSpec((None, None, tf, d), lambda b, f, be, br, nv: (l, be[b], f, 0)),
            pl.BlockSpec((None, None, 1, d), lambda b, f, be, br, nv: (l, be[b], 0, 0)),
        ],
        out_specs=pl.BlockSpec((MOE_BLOCK * TOK_ROWS, LANES), lambda b, f, be, br, nv: (b, 0)),
        scratch_shapes=[pltpu.VMEM((MOE_BLOCK, d), BF16), pltpu.VMEM((MOE_BLOCK, d), F32)],
    )
    return pl.pallas_call(
        _experts_kernel,
        grid_spec=grid_spec,
        out_shape=jax.ShapeDtypeStruct((rows * TOK_ROWS, LANES), I32),
        compiler_params=_params(("arbitrary", "arbitrary")),
        name="experts",
    )(block_e, block_row, block_valid, xs, lw["exp_w_up"], lw["exp_w_up"], lw["exp_b_up"], lw["exp_b_up"],
      lw["exp_w_down"], lw["exp_b_down"])


def _combine_kernel(dest_ref, ys_ref, gates_ref, h_ref, g_ref, b_ref, h32_ref, h16_ref, buf_ref, sem, *, alpha):
    tm = h_ref.shape[0]

    def row_copy(r, k):
        return pltpu.make_async_copy(_token_tile(ys_ref, dest_ref[r * TOP_K + k]), _token_tile(buf_ref.at[k], r), sem)

    def start(r, c):
        for k in range(TOP_K):
            row_copy(r, k).start(priority=k % 2)
        return c

    def wait(r, c):
        for k in range(TOP_K):
            row_copy(r, k).wait()
        return c

    lax.fori_loop(0, tm, start, 0, unroll=4)
    lax.fori_loop(0, tm, wait, 0, unroll=4)
    gates = gates_ref[...]
    los, his = [], []
    for c in range(TOK_ROWS):
        lo_c = hi_c = None
        for k in range(TOP_K):
            lo, hi = _unpack_halves(_load_token_tiles(buf_ref, tm, c, lead=(k,)))
            gk = gates[:, k:k + 1]
            lo_c = lo * gk if lo_c is None else lo_c + lo * gk
            hi_c = hi * gk if hi_c is None else hi_c + hi * gk
        los.append(lo_c)
        his.append(hi_c)
    moe = jnp.concatenate(los + his, axis=-1)
    y = _layer_norm(alpha * h_ref[...] + moe, g_ref[...], b_ref[...])
    h32_ref[...] = y
    h16_ref[...] = y.astype(BF16)


def _combine(ys, dest_flat, gates, h32, g, b, l, alpha):
    n, d = h32.shape
    return pl.pallas_call(
        functools.partial(_combine_kernel, alpha=alpha),
        grid=(n // ROW_TILE,),
        in_specs=[
            pl.BlockSpec((ROW_TILE * TOP_K,), lambda i: (i,), memory_space=pltpu.SMEM),
            pl.BlockSpec(memory_space=pl.ANY),
            pl.BlockSpec((ROW_TILE, LANES), lambda i: (i, 0)),
            pl.BlockSpec((ROW_TILE, d), lambda i: (i, 0)),
            pl.BlockSpec((None, 1, d), lambda i: (l, 0, 0)),
            pl.BlockSpec((None, 1, d), lambda i: (l, 0, 0)),
        ],
        out_specs=[pl.BlockSpec((ROW_TILE, d), lambda i: (i, 0)),
                   pl.BlockSpec((ROW_TILE, d), lambda i: (i, 0))],
        out_shape=[jax.ShapeDtypeStruct((n, d), F32), jax.ShapeDtypeStruct((n, d), BF16)],
        scratch_shapes=[pltpu.VMEM((TOP_K, ROW_TILE * TOK_ROWS, LANES), I32), pltpu.SemaphoreType.DMA(())],
        compiler_params=_params(("arbitrary",)),
        name="combine",
    )(dest_flat, ys, gates, h32, g, b)


def _routing_tables(route, counts_f, n_experts, n_blocks):
    idx = route[:, :TOP_K]
    rank = route[:, TOP_K:2 * TOP_K]
    counts = counts_f[0, :n_experts].astype(I32)
    padded = (counts + MOE_BLOCK - 1) // MOE_BLOCK * MOE_BLOCK
    pad_end = jnp.cumsum(padded)
    pad_start = pad_end - padded
    dest = (pad_start[idx] + rank).reshape(-1).astype(I32)
    n_used = pad_end[-1] // MOE_BLOCK
    blk = jnp.minimum(jnp.arange(n_blocks, dtype=I32), n_used - 1)
    block_e = jnp.sum((pad_end[None, :] <= (blk * MOE_BLOCK)[:, None]).astype(I32), axis=1)
    block_e = jnp.minimum(block_e, n_experts - 1)
    in_use = jnp.arange(n_blocks, dtype=I32) < n_used
    valid = jnp.clip(counts[block_e] - (blk * MOE_BLOCK - pad_start[block_e]), 0, MOE_BLOCK)
    valid = jnp.where(in_use, valid, 0).astype(I32)
    zero_lo = jnp.concatenate([pad_start + counts, pad_end[-1:]]).astype(I32)
    zero_hi = jnp.concatenate([pad_end, jnp.full((1,), n_blocks * MOE_BLOCK, I32)]).astype(I32)
    return dest, block_e.astype(I32), blk.astype(I32), valid, zero_lo, zero_hi


def _pad_state(state, rows):
    return jnp.pad(state, ((0, 0), (0, 0), (rows - state.shape[2], 0), (0, 0)))


def kernel(x_prompt, x_sample, state_pool, state_conv, ln_in_g, ln_in_b, w_in, b_in, pool_w, pool_scale, conv_w,
           gm_ln_g, gm_ln_b, gm_ws, gm_bs, w_branch, w_out, ln1_g, ln1_b, router_w, router_b, exp_w_up,
           exp_b_up, exp_w_down, exp_b_down, ln2_g, ln2_b):
    bp, seq, d = x_prompt.shape
    bs, dseq, _ = x_sample.shape
    depth = w_in.shape[0]
    d_br = pool_scale.shape[1]
    n_experts = router_w.shape[2]
    alpha = float((2 * depth) ** 0.25)
    n_p, n_s = bp * seq, bs * dseq
    n = n_p + n_s
    n_blocks = (n * TOP_K) // MOE_BLOCK + n_experts
    pool_state = POOL_WINDOWS[-1] - 1

    rw = jnp.pad(router_w, ((0, 0), (0, 0), (0, LANES - n_experts)))
    rwh = rw.astype(BF16)
    lw = dict(
        pool_w=pool_w.astype(BF16), pool_scale=pool_scale.reshape(depth, 1, d_br), conv_w=conv_w,
        gm_ln_g=gm_ln_g.reshape(depth, 1, d_br), gm_ln_b=gm_ln_b.reshape(depth, 1, d_br),
        gm_ws=gm_ws.astype(BF16), gm_bst=jnp.swapaxes(gm_bs, 1, 2), w_branch=w_branch.astype(BF16),
        w_out=w_out.astype(BF16), ln1_g=ln1_g.reshape(depth, 1, d), ln1_b=ln1_b.reshape(depth, 1, d),
        router_wh=rwh, router_wl=(rw - rwh.astype(F32)).astype(BF16),
        router_b=jnp.pad(router_b, ((0, 0), (0, LANES - n_experts))).reshape(depth, 1, LANES),
        exp_w_up=exp_w_up, exp_b_up=exp_b_up.reshape(depth, n_experts, 1, -1),
        exp_w_down=exp_w_down, exp_b_down=exp_b_down.reshape(depth, n_experts, 1, d),
    )
    w_in16 = w_in.astype(BF16)
    b_in3 = b_in.reshape(depth, 1, -1)
    ln2_g3, ln2_b3 = ln2_g.reshape(depth, 1, d), ln2_b.reshape(depth, 1, d)
    sp16 = _pad_state(state_pool, POOL_HALO)
    sc16 = _pad_state(state_conv, POOL_HALO)
    zero_state = jnp.zeros((bp, POOL_HALO, d_br), F32)

    h32, h16 = _ln_in(x_prompt.reshape(n_p, d), x_sample.reshape(n_s, d), ln_in_g, ln_in_b)
    new_pool_p, new_conv_p, new_pool_s, new_conv_s, v_rows = [], [], [], [], []
    for l in range(depth):
        proj = _proj(h16, w_in16, b_in3, l)
        mp, ptail_p, ctail_p = _mixer(proj, 0, bp, seq, MIX_TILE, 0, zero_state, zero_state, lw, l, False)
        ms, ptail_s, ctail_s, vn = _mixer(proj, n_p, bs, dseq, dseq, PAST_LEN, sp16[l], sc16[l], lw, l, True)
        h1, hpk, route, gates, counts = _outproj(mp, ms, h32, lw, l, alpha, n_experts)
        dest, block_e, block_row, block_valid, zero_lo, zero_hi = _routing_tables(route, counts, n_experts, n_blocks)
        xs = _dispatch(hpk, dest, zero_lo, zero_hi, n_blocks * MOE_BLOCK)
        ys = _experts(xs, block_e, block_row, block_valid, lw, l)
        h32, h16 = _combine(ys, dest, gates, h1, ln2_g3, ln2_b3, l, alpha)
        new_pool_p.append(ptail_p[:, POOL_HALO - pool_state:, :])
        new_conv_p.append(ctail_p[:, CONV_TAIL - (CONV_WIDTH - 1):, :])
        new_pool_s.append(ptail_s[:, POOL_HALO - pool_state:, :])
        new_conv_s.append(ctail_s[:, CONV_TAIL - (CONV_WIDTH - 1):, :])
        v_rows.append(vn.reshape(bs, dseq, d_br))
    y_prompt = h32[:n_p].reshape(bp, seq, d)
    y_sample = h32[n_p:].reshape(bs, dseq, d)
    return (y_prompt, y_sample, jnp.stack(new_pool_p), jnp.stack(new_conv_p), jnp.stack(new_pool_s),
            jnp.stack(new_conv_s), jnp.stack(v_rows))
```

```python
import functools

import jax
import jax.numpy as jnp
from jax import lax
from jax.experimental import pallas as pl
from jax.experimental.pallas import tpu as pltpu

F32 = jnp.float32
BF16 = jnp.bfloat16
I32 = jnp.int32

POOL_WINDOWS = (2, 4, 8, 16)
POOL_HALO = 16
CONV_WIDTH = 3
CONV_TAIL = 8
GM_CHUNK = 128
GM_HEADS = 8
N_BRANCH = 3
TOP_K = 4
SWIGLU_LIMIT = 7.0
SWIGLU_ALPHA = 1.702
LN_EPS = 1e-5
PAST_LEN = 4096
LANES = 128

ROW_TILE = 256
MIX_TILE = 256
MOE_BLOCK = 1280
MOE_FF_TILE = 512
ZERO_CHUNK = 64
VMEM_LIMIT = 60 * 1024 * 1024
MOE_SUB = 256
TOK_ROWS = 8
HIGH_HALF = -65536


def _params(sem, vmem=VMEM_LIMIT):
    return pltpu.CompilerParams(dimension_semantics=sem, vmem_limit_bytes=vmem)


def _layer_norm(x, g, b):
    mu = jnp.mean(x, axis=-1, keepdims=True)
    xc = x - mu
    var = jnp.mean(xc * xc, axis=-1, keepdims=True)
    return xc * lax.rsqrt(var + LN_EPS) * g + b


def _resident(shape, index_map):
    return pl.BlockSpec(shape, index_map, pipeline_mode=pl.Buffered(1))


def _pack_halves(y):
    half = y.shape[1] // 2
    bits = pltpu.bitcast(y.astype(BF16).astype(F32), I32)
    return lax.shift_right_logical(bits[:, :half], 16) | bits[:, half:]


def _unpack_halves(w):
    return pltpu.bitcast(w << 16, F32), pltpu.bitcast(w & HIGH_HALF, F32)


def _store_token_tiles(ref, packed, lead=()):
    m = packed.shape[0]
    for c in range(TOK_ROWS):
        ref[lead + (pl.ds(c, m, stride=TOK_ROWS), slice(None))] = packed[:, c * LANES:(c + 1) * LANES]


def _load_token_tiles(ref, m, c, lead=()):
    return ref[lead + (pl.ds(c, m, stride=TOK_ROWS), slice(None))]


def _ln_in_kernel(xp_ref, xs_ref, g_ref, b_ref, h32_ref, h16_ref, *, n_prompt_tiles):
    i = pl.program_id(0)
    x = jnp.where(i < n_prompt_tiles, xp_ref[...], xs_ref[...])
    y = _layer_norm(x, g_ref[...], b_ref[...])
    h32_ref[...] = y
    h16_ref[...] = y.astype(BF16)


def _ln_in(xp, xs, g, b):
    n_p, d = xp.shape
    n_s = xs.shape[0]
    tp, ts = n_p // ROW_TILE, n_s // ROW_TILE
    n = n_p + n_s
    return pl.pallas_call(
        functools.partial(_ln_in_kernel, n_prompt_tiles=tp),
        grid=(tp + ts,),
        in_specs=[
            pl.BlockSpec((ROW_TILE, d), lambda i: (jnp.minimum(i, tp - 1), 0)),
            pl.BlockSpec((ROW_TILE, d), lambda i: (jnp.maximum(i - tp, 0), 0)),
            pl.BlockSpec((1, d), lambda i: (0, 0)),
            pl.BlockSpec((1, d), lambda i: (0, 0)),
        ],
        out_specs=[pl.BlockSpec((ROW_TILE, d), lambda i: (i, 0)),
                   pl.BlockSpec((ROW_TILE, d), lambda i: (i, 0))],
        out_shape=[jax.ShapeDtypeStruct((n, d), F32), jax.ShapeDtypeStruct((n, d), BF16)],
        compiler_params=_params(("arbitrary",)),
        name="ln_in",
    )(xp, xs, g.reshape(1, d), b.reshape(1, d))


def _proj_kernel(x_ref, w_ref, b_ref, o_ref):
    o_ref[...] = (jnp.dot(x_ref[...], w_ref[...], preferred_element_type=F32) + b_ref[...]).astype(o_ref.dtype)


def _pick_tile(n, candidates):
    for c in candidates:
        if n % c == 0:
            return c
    raise ValueError(f"no tile in {candidates} divides {n}")


def _proj(h16, w16, b, l):
    n, d = h16.shape
    d_in = w16.shape[2]
    bm = _pick_tile(n, (1280, 1024, 512, 256))
    bn = 1024
    return pl.pallas_call(
        _proj_kernel,
        grid=(n // bm, d_in // bn),
        in_specs=[
            pl.BlockSpec((bm, d), lambda i, j: (i, 0)),
            pl.BlockSpec((None, d, bn), lambda i, j: (l, 0, j)),
            pl.BlockSpec((None, 1, bn), lambda i, j: (l, 0, j)),
        ],
        out_specs=pl.BlockSpec((bm, bn), lambda i, j: (i, j)),
        out_shape=jax.ShapeDtypeStruct((n, d_in), BF16),
        compiler_params=_params(("arbitrary", "arbitrary")),
        name="proj",
    )(h16, w16, b)


def _mixer_kernel(gate_ref, p_ref, xc_ref, bg_ref, cg_ref, u_ref, v_ref, hp_ref, hxc_ref, hcg_ref,
                  sp_ref, sc_ref, poolw_ref, pscale_ref, convw_ref, lng_ref, lnb_ref, ws_ref, bst_ref,
                  wbr_ref, *rest, tt, pos0, want_v):
    if want_v:
        merged_ref, ptail_ref, ctail_ref, vn_ref, extp_ref, extz_ref, vpad_ref = rest
    else:
        merged_ref, ptail_ref, ctail_ref, extp_ref, extz_ref, vpad_ref = rest
        vn_ref = None
    i = pl.program_id(1)
    first = i == 0
    d_br = p_ref.shape[1]
    d_model = merged_ref.shape[1]
    gdim = d_br // len(POOL_WINDOWS)

    p = p_ref[...].astype(F32)
    extp_ref[0:POOL_HALO, :] = jnp.where(first, sp_ref[0], hp_ref[...].astype(F32))
    extp_ref[POOL_HALO:POOL_HALO + tt, :] = p
    pos = lax.broadcasted_iota(I32, (tt, 1), 0) + (i * tt + pos0)
    a_parts = []
    for g, win in enumerate(POOL_WINDOWS):
        cols = slice(g * gdim, (g + 1) * gdim)
        s = p[:, cols]
        for j in range(1, win):
            s = s + extp_ref[POOL_HALO - j:POOL_HALO - j + tt, cols]
        cnt = jnp.minimum(pos + 1, win).astype(F32)
        mixed = s / cnt - p[:, cols]
        a_parts.append(jnp.dot(mixed.astype(BF16), poolw_ref[g], preferred_element_type=F32))
    a_out = jnp.concatenate(a_parts, axis=-1) * pscale_ref[...]

    z = cg_ref[...].astype(F32) * xc_ref[...].astype(F32)
    extz_ref[0:POOL_HALO, :] = jnp.where(first, sc_ref[0], hcg_ref[...].astype(F32) * hxc_ref[...].astype(F32))
    extz_ref[POOL_HALO:POOL_HALO + tt, :] = z
    y = z * convw_ref[CONV_WIDTH - 1:CONV_WIDTH, :]
    for k in range(CONV_WIDTH - 1):
        off = POOL_HALO - (CONV_WIDTH - 1) + k
        y = y + extz_ref[off:off + tt, :] * convw_ref[k:k + 1, :]
    c_out = bg_ref[...].astype(F32) * y

    vn = _layer_norm(v_ref[...].astype(F32), lng_ref[...], lnb_ref[...])
    if want_v:
        vn_ref[...] = vn
    hd = d_br // GM_HEADS
    tri = (lax.broadcasted_iota(I32, (GM_CHUNK, GM_CHUNK), 0)
           >= lax.broadcasted_iota(I32, (GM_CHUNK, GM_CHUNK), 1))
    wms = [jnp.where(tri, ws_ref[h], jnp.zeros((), BF16)) for h in range(GM_HEADS)]
    s_chunks = []
    for c in range(max(tt // GM_CHUNK, 1)):
        if tt < GM_CHUNK:
            vpad_ref[...] = jnp.zeros(vpad_ref.shape, BF16)
            vpad_ref[0:tt, :] = vn.astype(BF16)
            vchunk = vpad_ref[...]
        else:
            vchunk = vn[c * GM_CHUNK:(c + 1) * GM_CHUNK, :].astype(BF16)
        heads = []
        for h in range(GM_HEADS):
            sh = jnp.dot(wms[h], vchunk[:, h * hd:(h + 1) * hd], preferred_element_type=F32)
            heads.append(sh + bst_ref[:, h:h + 1])
        s_chunks.append(jnp.concatenate(heads, axis=-1)[:min(tt, GM_CHUNK), :])
    s_all = s_chunks[0] if len(s_chunks) == 1 else jnp.concatenate(s_chunks, axis=0)
    g_out = u_ref[...].astype(F32) * s_all

    merged = None
    for n, br in enumerate((a_out, c_out, g_out)):
        t = jnp.dot(br.astype(BF16), wbr_ref[n], preferred_element_type=F32)
        t = jax.nn.sigmoid(gate_ref[:, n * d_model:(n + 1) * d_model].astype(F32)) * t
        merged = t if merged is None else merged + t
    merged_ref[...] = merged.astype(merged_ref.dtype)
    ptail_ref[0] = p[tt - POOL_HALO:tt, :]
    ctail_ref[0] = z[tt - CONV_TAIL:tt, :]


def _mixer(proj, row_off, nb, t, tt, pos0, sp16, sc16, lw, l, want_v):
    d_br = sp16.shape[2]
    d_model = lw["w_branch"].shape[3]
    gate_w = N_BRANCH * d_model
    col0 = gate_w // d_br
    nt = t // tt
    base = row_off // tt

    def row(b, i):
        return base + b * nt + i

    def halo(b, i):
        return jnp.maximum((row_off + b * t + i * tt) // POOL_HALO - 1, 0)

    in_specs = [pl.BlockSpec((tt, gate_w), lambda b, i: (row(b, i), 0))]
    in_specs += [pl.BlockSpec((tt, d_br), functools.partial(lambda b, i, c: (row(b, i), col0 + c), c=c))
                 for c in range(6)]
    in_specs += [pl.BlockSpec((POOL_HALO, d_br), functools.partial(lambda b, i, c: (halo(b, i), col0 + c), c=c))
                 for c in (0, 1, 3)]
    in_specs += [pl.BlockSpec((1, POOL_HALO, d_br), lambda b, i: (b, 0, 0)),
                 pl.BlockSpec((1, POOL_HALO, d_br), lambda b, i: (b, 0, 0))]
    gdim = d_br // len(POOL_WINDOWS)
    in_specs += [
        _resident((None, len(POOL_WINDOWS), gdim, gdim), lambda b, i: (l, 0, 0, 0)),
        _resident((None, 1, d_br), lambda b, i: (l, 0, 0)),
        _resident((None, CONV_WIDTH, d_br), lambda b, i: (l, 0, 0)),
        _resident((None, 1, d_br), lambda b, i: (l, 0, 0)),
        _resident((None, 1, d_br), lambda b, i: (l, 0, 0)),
        _resident((None, GM_HEADS, GM_CHUNK, GM_CHUNK), lambda b, i: (l, 0, 0, 0)),
        _resident((None, GM_CHUNK, GM_HEADS), lambda b, i: (l, 0, 0)),
        _resident((None, N_BRANCH, d_br, d_model), lambda b, i: (l, 0, 0, 0)),
    ]
    out_specs = [pl.BlockSpec((tt, d_model), lambda b, i: (b * nt + i, 0)),
                 pl.BlockSpec((1, POOL_HALO, d_br), lambda b, i: (b, 0, 0)),
                 pl.BlockSpec((1, CONV_TAIL, d_br), lambda b, i: (b, 0, 0))]
    out_shape = [jax.ShapeDtypeStruct((nb * t, d_model), BF16),
                 jax.ShapeDtypeStruct((nb, POOL_HALO, d_br), F32),
                 jax.ShapeDtypeStruct((nb, CONV_TAIL, d_br), F32)]
    if want_v:
        out_specs.append(pl.BlockSpec((tt, d_br), lambda b, i: (b * nt + i, 0)))
        out_shape.append(jax.ShapeDtypeStruct((nb * t, d_br), F32))
    return pl.pallas_call(
        functools.partial(_mixer_kernel, tt=tt, pos0=pos0, want_v=want_v),
        grid=(nb, nt),
        in_specs=in_specs,
        out_specs=out_specs,
        out_shape=out_shape,
        scratch_shapes=[pltpu.VMEM((POOL_HALO + tt, d_br), F32),
                        pltpu.VMEM((POOL_HALO + tt, d_br), F32),
                        pltpu.VMEM((GM_CHUNK, d_br), BF16)],
        compiler_params=_params(("arbitrary", "arbitrary")),
        name=f"mixer_t{t}",
    )(proj, proj, proj, proj, proj, proj, proj, proj, proj, proj, sp16, sc16,
      lw["pool_w"], lw["pool_scale"], lw["conv_w"], lw["gm_ln_g"], lw["gm_ln_b"], lw["gm_ws"], lw["gm_bst"],
      lw["w_branch"])


def _outproj_kernel(mp_ref, ms_ref, h_ref, w_ref, g_ref, b_ref, rwh_ref, rwl_ref, rb_ref,
                    h32_ref, hpk_ref, route_ref, gates_ref, counts_ref, carry_ref,
                    *, n_prompt_tiles, alpha, n_experts):
    i = pl.program_id(0)
    tm = h_ref.shape[0]

    @pl.when(i == 0)
    def _():
        carry_ref[...] = jnp.zeros(carry_ref.shape, F32)

    m = jnp.where(i < n_prompt_tiles, mp_ref[...], ms_ref[...])
    y = alpha * h_ref[...] + jnp.dot(m, w_ref[...], preferred_element_type=F32)
    h1 = _layer_norm(y, g_ref[...], b_ref[...])
    h32_ref[...] = h1
    _store_token_tiles(hpk_ref, _pack_halves(h1))

    hh = h1.astype(BF16)
    hl = (h1 - hh.astype(F32)).astype(BF16)
    logits = (jnp.dot(hh, rwh_ref[...], preferred_element_type=F32)
              + jnp.dot(hl, rwh_ref[...], preferred_element_type=F32)
              + jnp.dot(hh, rwl_ref[...], preferred_element_type=F32)) + rb_ref[...]
    lane = lax.broadcasted_iota(I32, (tm, LANES), 1)
    lane_f = lane.astype(F32)
    neg = jnp.float32(-jnp.inf)
    cur = jnp.where(lane < n_experts, logits, neg)
    vals, idxs = [], []
    for _ in range(TOP_K):
        mx = jnp.max(cur, axis=-1, keepdims=True)
        ik = jnp.min(jnp.where(cur == mx, lane_f, float(LANES)), axis=-1, keepdims=True).astype(I32)
        vals.append(mx)
        idxs.append(ik)
        cur = jnp.where(lane == ik, neg, cur)
    exps = [jnp.exp(v - vals[0]) for v in vals]
    denom = exps[0]
    for e in exps[1:]:
        denom = denom + e

    onehot = jnp.zeros((tm, LANES), F32)
    for ik in idxs:
        onehot = onehot + (lane == ik).astype(F32)
    below = (lax.broadcasted_iota(I32, (tm, tm), 0) > lax.broadcasted_iota(I32, (tm, tm), 1)).astype(BF16)
    prefix = jnp.dot(below, onehot.astype(BF16), preferred_element_type=F32) + carry_ref[...]
    route = jnp.zeros((tm, LANES), I32)
    gates = jnp.zeros((tm, LANES), F32)
    for k in range(TOP_K):
        rank = jnp.sum(jnp.where(lane == idxs[k], prefix, 0.0), axis=-1, keepdims=True).astype(I32)
        route = jnp.where(lane == k, idxs[k], route)
        route = jnp.where(lane == TOP_K + k, rank, route)
        gates = jnp.where(lane == k, exps[k] / denom, gates)
    route_ref[...] = route
    gates_ref[...] = gates
    carry_ref[...] = carry_ref[...] + jnp.sum(onehot, axis=0, keepdims=True)
    counts_ref[...] = jnp.broadcast_to(carry_ref[...], counts_ref.shape)


def _outproj(mp, ms, h32, lw, l, alpha, n_experts):
    n, d = h32.shape
    tp, ts = mp.shape[0] // ROW_TILE, ms.shape[0] // ROW_TILE
    return pl.pallas_call(
        functools.partial(_outproj_kernel, n_prompt_tiles=tp, alpha=alpha, n_experts=n_experts),
        grid=(tp + ts,),
        in_specs=[
            pl.BlockSpec((ROW_TILE, d), lambda i: (jnp.minimum(i, tp - 1), 0)),
            pl.BlockSpec((ROW_TILE, d), lambda i: (jnp.maximum(i - tp, 0), 0)),
            pl.BlockSpec((ROW_TILE, d), lambda i: (i, 0)),
            _resident((None, d, d), lambda i: (l, 0, 0)),
            _resident((None, 1, d), lambda i: (l, 0, 0)),
            _resident((None, 1, d), lambda i: (l, 0, 0)),
            _resident((None, d, LANES), lambda i: (l, 0, 0)),
            _resident((None, d, LANES), lambda i: (l, 0, 0)),
            _resident((None, 1, LANES), lambda i: (l, 0, 0)),
        ],
        out_specs=[pl.BlockSpec((ROW_TILE, d), lambda i: (i, 0)),
                   pl.BlockSpec((ROW_TILE * TOK_ROWS, LANES), lambda i: (i, 0)),
                   pl.BlockSpec((ROW_TILE, LANES), lambda i: (i, 0)),
                   pl.BlockSpec((ROW_TILE, LANES), lambda i: (i, 0)),
                   pl.BlockSpec((8, LANES), lambda i: (0, 0))],
        out_shape=[jax.ShapeDtypeStruct((n, d), F32),
                   jax.ShapeDtypeStruct((n * TOK_ROWS, LANES), I32),
                   jax.ShapeDtypeStruct((n, LANES), I32),
                   jax.ShapeDtypeStruct((n, LANES), F32),
                   jax.ShapeDtypeStruct((8, LANES), F32)],
        scratch_shapes=[pltpu.VMEM((1, LANES), F32)],
        compiler_params=_params(("arbitrary",)),
        name="outproj_router",
    )(mp, ms, h32, lw["w_out"], lw["ln1_g"], lw["ln1_b"], lw["router_wh"], lw["router_wl"], lw["router_b"])


def _token_tile(ref, row):
    return ref.at[pl.ds(pl.multiple_of(row * TOK_ROWS, TOK_ROWS), TOK_ROWS)]


def _dispatch_kernel(zlo_ref, zhi_ref, dest_ref, h_ref, xs_ref, zero_ref, sem, zsem):
    tm = h_ref.shape[0] // TOK_ROWS

    @pl.when(pl.program_id(0) == 0)
    def _():
        zero_ref[...] = jnp.zeros(zero_ref.shape, I32)

        def bulk_copy(row):
            dst = xs_ref.at[pl.ds(pl.multiple_of(row * TOK_ROWS, TOK_ROWS), ZERO_CHUNK * TOK_ROWS)]
            return pltpu.make_async_copy(zero_ref, dst, zsem)

        def tile_copy(row):
            return pltpu.make_async_copy(zero_ref.at[pl.ds(0, TOK_ROWS)], _token_tile(xs_ref, row), zsem)

        def each_range(g, c):
            lo, hi = zlo_ref[g], zhi_ref[g]
            n_bulk = (hi - lo) // ZERO_CHUNK
            rest = lo + n_bulk * ZERO_CHUNK
            lax.fori_loop(0, n_bulk, lambda j, cc: (bulk_copy(lo + j * ZERO_CHUNK).start(), cc)[1], 0)
            lax.fori_loop(rest, hi, lambda row, cc: (tile_copy(row).start(), cc)[1], 0)
            lax.fori_loop(0, n_bulk, lambda j, cc: (bulk_copy(lo + j * ZERO_CHUNK).wait(), cc)[1], 0)
            lax.fori_loop(rest, hi, lambda row, cc: (tile_copy(row).wait(), cc)[1], 0)
            return c

        lax.fori_loop(0, zlo_ref.shape[0], each_range, 0)

    def row_copy(r, k):
        return pltpu.make_async_copy(_token_tile(h_ref, r), _token_tile(xs_ref, dest_ref[r * TOP_K + k]), sem)

    def start(r, c):
        for k in range(TOP_K):
            row_copy(r, k).start(priority=k % 2)
        return c

    def wait(r, c):
        for k in range(TOP_K):
            row_copy(r, k).wait()
        return c

    lax.fori_loop(0, tm, start, 0, unroll=4)
    lax.fori_loop(0, tm, wait, 0, unroll=4)


def _dispatch(hpk, dest_flat, zero_lo, zero_hi, n_rows):
    n = hpk.shape[0] // TOK_ROWS
    grid_spec = pltpu.PrefetchScalarGridSpec(
        num_scalar_prefetch=2,
        grid=(n // ROW_TILE,),
        in_specs=[
            pl.BlockSpec((ROW_TILE * TOP_K,), lambda i, zl, zh: (i,), memory_space=pltpu.SMEM),
            pl.BlockSpec((ROW_TILE * TOK_ROWS, LANES), lambda i, zl, zh: (i, 0)),
        ],
        out_specs=pl.BlockSpec(memory_space=pl.ANY),
        scratch_shapes=[pltpu.VMEM((ZERO_CHUNK * TOK_ROWS, LANES), I32), pltpu.SemaphoreType.DMA(()),
                        pltpu.SemaphoreType.DMA(())],
    )
    return pl.pallas_call(
        _dispatch_kernel,
        grid_spec=grid_spec,
        out_shape=jax.ShapeDtypeStruct((n_rows * TOK_ROWS, LANES), I32),
        compiler_params=_params(("arbitrary",)),
        name="dispatch",
    )(zero_lo, zero_hi, dest_flat, hpk)


def _experts_kernel(be_ref, br_ref, nv_ref, x_ref, wg_ref, wl_ref, bg_ref, bl_ref, wd_ref, bd_ref, o_ref,
                    x16_ref, acc_ref):
    del be_ref, br_ref
    b = pl.program_id(0)
    f = pl.program_id(1)
    nf = pl.num_programs(1)
    n_valid = nv_ref[b]
    tm, d = x16_ref.shape
    half = d // 2

    @pl.when(n_valid > 0)
    def _():
        @pl.when(f == 0)
        def _():
            for c in range(TOK_ROWS):
                lo, hi = _unpack_halves(_load_token_tiles(x_ref, tm, c))
                x16_ref[:, c * LANES:(c + 1) * LANES] = lo.astype(BF16)
                x16_ref[:, half + c * LANES:half + (c + 1) * LANES] = hi.astype(BF16)
            acc_ref[...] = jnp.broadcast_to(bd_ref[...], acc_ref.shape)

        def run(groups):
            wg16 = wg_ref[...].astype(BF16)
            wl16 = wl_ref[...].astype(BF16)
            wd16 = wd_ref[...].astype(BF16)
            row0 = 0
            for cm in groups:
                rows = slice(row0, row0 + cm)
                row0 += cm
                x = x16_ref[rows, :]
                hg = jnp.dot(x, wg16, preferred_element_type=F32) + bg_ref[...]
                hl = jnp.dot(x, wl16, preferred_element_type=F32) + bl_ref[...]
                glu = jnp.minimum(hg, SWIGLU_LIMIT)
                lin = jnp.clip(hl, -SWIGLU_LIMIT, SWIGLU_LIMIT)
                a = glu * jax.nn.sigmoid(SWIGLU_ALPHA * glu) * (lin + 1.0)
                acc_ref[rows, :] += jnp.dot(a.astype(BF16), wd16, preferred_element_type=F32)

        n_sub = (n_valid + (MOE_SUB - 1)) // MOE_SUB
        for q in range(1, tm // MOE_SUB + 1):
            groups = ((q - 1) * MOE_SUB, MOE_SUB) if q > 1 else (MOE_SUB,)
            pl.when(n_sub == q)(functools.partial(run, groups))

        @pl.when(f == nf - 1)
        def _():
            _store_token_tiles(o_ref, _pack_halves(acc_ref[...]))

    @pl.when(jnp.logical_and(n_valid == 0, f == 0))
    def _():
        o_ref[...] = jnp.zeros(o_ref.shape, I32)


def _experts(xs, block_e, block_row, block_valid, lw, l):
    rows = xs.shape[0] // TOK_ROWS
    d = lw["exp_w_down"].shape[3]
    d_ff = lw["exp_w_up"].shape[3] // 2
    nf = d_ff // MOE_FF_TILE
    n_blocks = rows // MOE_BLOCK
    tf = MOE_FF_TILE
    grid_spec = pltpu.PrefetchScalarGridSpec(
        num_scalar_prefetch=3,
        grid=(n_blocks, nf),
        in_specs=[
            pl.BlockSpec((MOE_BLOCK * TOK_ROWS, LANES), lambda b, f, be, br, nv: (br[b], 0),
                         pipeline_mode=pl.Buffered(1)),
            pl.BlockSpec((None, None, d, tf), lambda b, f, be, br, nv: (l, be[b], 0, f)),
            pl.BlockSpec((None, None, d, tf), lambda b, f, be, br, nv: (l, be[b], 0, nf + f)),
            pl.BlockSpec((None, None, 1, tf), lambda b, f, be, br, nv: (l, be[b], 0, f)),
            pl.BlockSpec((None, None, 1, tf), lambda b, f, be, br, nv: (l, be[b], 0, nf + f)),
            pl.BlockSpec((None, None, tf, d), lambda b, f, be, br, nv: (l, be[b], f, 0)),
            pl.BlockSpec((None, None, 1, d), lambda b, f, be, br, nv: (l, be[b], 0, 0)),
        ],
        out_specs=pl.BlockSpec((MOE_BLOCK * TOK_ROWS, LANES), lambda b, f, be, br, nv: (b, 0),
                               pipeline_mode=pl.Buffered(1)),
        scratch_shapes=[pltpu.VMEM((MOE_BLOCK, d), BF16), pltpu.VMEM((MOE_BLOCK, d), F32)],
    )
    return pl.pallas_call(
        _experts_kernel,
        grid_spec=grid_spec,
        out_shape=jax.ShapeDtypeStruct((rows * TOK_ROWS, LANES), I32),
        compiler_params=_params(("arbitrary", "arbitrary")),
        name="experts",
    )(block_e, block_row, block_valid, xs, lw["exp_w_up"], lw["exp_w_up"], lw["exp_b_up"], lw["exp_b_up"],
      lw["exp_w_down"], lw["exp_b_down"])


def _combine_kernel(dest_ref, ys_ref, gates_ref, h_ref, g_ref, b_ref, h32_ref, h16_ref, buf_ref, sem, *, alpha):
    tm = h_ref.shape[0]

    def row_copy(r, k):
        return pltpu.make_async_copy(_token_tile(ys_ref, dest_ref[r * TOP_K + k]), _token_tile(buf_ref.at[k], r), sem)

    def start(r, c):
        for k in range(TOP_K):
            row_copy(r, k).start(priority=k % 2)
        return c

    def wait(r, c):
        for k in range(TOP_K):
            row_copy(r, k).wait()
        return c

    lax.fori_loop(0, tm, start, 0, unroll=4)
    lax.fori_loop(0, tm, wait, 0, unroll=4)
    gates = gates_ref[...]
    los, his = [], []
    for c in range(TOK_ROWS):
        lo_c = hi_c = None
        for k in range(TOP_K):
            lo, hi = _unpack_halves(_load_token_tiles(buf_ref, tm, c, lead=(k,)))
            gk = gates[:, k:k + 1]
            lo_c = lo * gk if lo_c is None else lo_c + lo * gk
            hi_c = hi * gk if hi_c is None else hi_c + hi * gk
        los.append(lo_c)
        his.append(hi_c)
    moe = jnp.concatenate(los + his, axis=-1)
    y = _layer_norm(alpha * h_ref[...] + moe, g_ref[...], b_ref[...])
    h32_ref[...] = y
    h16_ref[...] = y.astype(BF16)


def _combine(ys, dest_flat, gates, h32, g, b, l, alpha):
    n, d = h32.shape
    return pl.pallas_call(
        functools.partial(_combine_kernel, alpha=alpha),
        grid=(n // ROW_TILE,),
        in_specs=[
            pl.BlockSpec((ROW_TILE * TOP_K,), lambda i: (i,), memory_space=pltpu.SMEM),
            pl.BlockSpec(memory_space=pl.ANY),
            pl.BlockSpec((ROW_TILE, LANES), lambda i: (i, 0)),
            pl.BlockSpec((ROW_TILE, d), lambda i: (i, 0)),
            pl.BlockSpec((None, 1, d), lambda i: (l, 0, 0)),
            pl.BlockSpec((None, 1, d), lambda i: (l, 0, 0)),
        ],
        out_specs=[pl.BlockSpec((ROW_TILE, d), lambda i: (i, 0)),
                   pl.BlockSpec((ROW_TILE, d), lambda i: (i, 0))],
        out_shape=[jax.ShapeDtypeStruct((n, d), F32), jax.ShapeDtypeStruct((n, d), BF16)],
        scratch_shapes=[pltpu.VMEM((TOP_K, ROW_TILE * TOK_ROWS, LANES), I32), pltpu.SemaphoreType.DMA(())],
        compiler_params=_params(("arbitrary",)),
        name="combine",
    )(dest_flat, ys, gates, h32, g, b)


def _routing_tables(route, counts_f, n_experts, n_blocks):
    idx = route[:, :TOP_K]
    rank = route[:, TOP_K:2 * TOP_K]
    counts = counts_f[0, :n_experts].astype(I32)
    padded = (counts + MOE_BLOCK - 1) // MOE_BLOCK * MOE_BLOCK
    pad_end = jnp.cumsum(padded)
    pad_start = pad_end - padded
    dest = (pad_start[idx] + rank).reshape(-1).astype(I32)
    n_used = pad_end[-1] // MOE_BLOCK
    blk = jnp.minimum(jnp.arange(n_blocks, dtype=I32), n_used - 1)
    block_e = jnp.sum((pad_end[None, :] <= (blk * MOE_BLOCK)[:, None]).astype(I32), axis=1)
    block_e = jnp.minimum(block_e, n_experts - 1)
    in_use = jnp.arange(n_blocks, dtype=I32) < n_used
    valid = jnp.clip(counts[block_e] - (blk * MOE_BLOCK - pad_start[block_e]), 0, MOE_BLOCK)
    valid = jnp.where(in_use, valid, 0).astype(I32)
    zero_lo = jnp.concatenate([pad_start + counts, pad_end[-1:]]).astype(I32)
    zero_hi = jnp.concatenate([pad_end, jnp.full((1,), n_blocks * MOE_BLOCK, I32)]).astype(I32)
    return dest, block_e.astype(I32), blk.astype(I32), valid, zero_lo, zero_hi


def _pad_state(state, rows):
    return jnp.pad(state, ((0, 0), (0, 0), (rows - state.shape[2], 0), (0, 0)))


def kernel(x_prompt, x_sample, state_pool, state_conv, ln_in_g, ln_in_b, w_in, b_in, pool_w, pool_scale, conv_w,
           gm_ln_g, gm_ln_b, gm_ws, gm_bs, w_branch, w_out, ln1_g, ln1_b, router_w, router_b, exp_w_up,
           exp_b_up, exp_w_down, exp_b_down, ln2_g, ln2_b):
    bp, seq, d = x_prompt.shape
    bs, dseq, _ = x_sample.shape
    depth = w_in.shape[0]
    d_br = pool_scale.shape[1]
    n_experts = router_w.shape[2]
    alpha = float((2 * depth) ** 0.25)
    n_p, n_s = bp * seq, bs * dseq
    n = n_p + n_s
    n_blocks = -(-(n * TOP_K) // MOE_BLOCK) + n_experts
    pool_state = POOL_WINDOWS[-1] - 1

    rw = jnp.pad(router_w, ((0, 0), (0, 0), (0, LANES - n_experts)))
    rwh = rw.astype(BF16)
    lw = dict(
        pool_w=pool_w.astype(BF16), pool_scale=pool_scale.reshape(depth, 1, d_br), conv_w=conv_w,
        gm_ln_g=gm_ln_g.reshape(depth, 1, d_br), gm_ln_b=gm_ln_b.reshape(depth, 1, d_br),
        gm_ws=gm_ws.astype(BF16), gm_bst=jnp.swapaxes(gm_bs, 1, 2), w_branch=w_branch.astype(BF16),
        w_out=w_out.astype(BF16), ln1_g=ln1_g.reshape(depth, 1, d), ln1_b=ln1_b.reshape(depth, 1, d),
        router_wh=rwh, router_wl=(rw - rwh.astype(F32)).astype(BF16),
        router_b=jnp.pad(router_b, ((0, 0), (0, LANES - n_experts))).reshape(depth, 1, LANES),
        exp_w_up=exp_w_up, exp_b_up=exp_b_up.reshape(depth, n_experts, 1, -1),
        exp_w_down=exp_w_down, exp_b_down=exp_b_down.reshape(depth, n_experts, 1, d),
    )
    w_in16 = w_in.astype(BF16)
    b_in3 = b_in.reshape(depth, 1, -1)
    ln2_g3, ln2_b3 = ln2_g.reshape(depth, 1, d), ln2_b.reshape(depth, 1, d)
    sp16 = _pad_state(state_pool, POOL_HALO)
    sc16 = _pad_state(state_conv, POOL_HALO)
    zero_state = jnp.zeros((bp, POOL_HALO, d_br), F32)

    h32, h16 = _ln_in(x_prompt.reshape(n_p, d), x_sample.reshape(n_s, d), ln_in_g, ln_in_b)
    new_pool_p, new_conv_p, new_pool_s, new_conv_s, v_rows = [], [], [], [], []
    for l in range(depth):
        proj = _proj(h16, w_in16, b_in3, l)
        mp, ptail_p, ctail_p = _mixer(proj, 0, bp, seq, MIX_TILE, 0, zero_state, zero_state, lw, l, False)
        ms, ptail_s, ctail_s, vn = _mixer(proj, n_p, bs, dseq, dseq, PAST_LEN, sp16[l], sc16[l], lw, l, True)
        h1, hpk, route, gates, counts = _outproj(mp, ms, h32, lw, l, alpha, n_experts)
        dest, block_e, block_row, block_valid, zero_lo, zero_hi = _routing_tables(route, counts, n_experts, n_blocks)
        xs = _dispatch(hpk, dest, zero_lo, zero_hi, n_blocks * MOE_BLOCK)
        ys = _experts(xs, block_e, block_row, block_valid, lw, l)
        h32, h16 = _combine(ys, dest, gates, h1, ln2_g3, ln2_b3, l, alpha)
        new_pool_p.append(ptail_p[:, POOL_HALO - pool_state:, :])
        new_conv_p.append(ctail_p[:, CONV_TAIL - (CONV_WIDTH - 1):, :])
        new_pool_s.append(ptail_s[:, POOL_HALO - pool_state:, :])
        new_conv_s.append(ctail_s[:, CONV_TAIL - (CONV_WIDTH - 1):, :])
        v_rows.append(vn.reshape(bs, dseq, d_br))
    y_prompt = h32[:n_p].reshape(bp, seq, d)
    y_sample = h32[n_p:].reshape(bs, dseq, d)
    return (y_prompt, y_sample, jnp.stack(new_pool_p), jnp.stack(new_conv_p), jnp.stack(new_pool_s),
            jnp.stack(new_conv_s), jnp.stack(v_rows))
```

```python
import functools

import jax
import jax.numpy as jnp
from jax import lax
from jax.experimental import pallas as pl
from jax.experimental.pallas import tpu as pltpu

F32 = jnp.float32
BF16 = jnp.bfloat16
I32 = jnp.int32

POOL_WINDOWS = (2, 4, 8, 16)
POOL_HALO = 16
CONV_WIDTH = 3
CONV_TAIL = 8
GM_CHUNK = 128
GM_HEADS = 8
N_BRANCH = 3
TOP_K = 4
SWIGLU_LIMIT = 7.0
SWIGLU_ALPHA = 1.702
LN_EPS = 1e-5
PAST_LEN = 4096
LANES = 128

ROW_TILE = 256
MIX_TILE = 256
MOE_BLOCK = 1024
MOE_FF_TILE = 512
ZERO_CHUNK = 64
VMEM_LIMIT = 60 * 1024 * 1024
MOE_SUB = 256
TOK_ROWS = 8
HIGH_HALF = -65536


def _params(sem, vmem=VMEM_LIMIT):
    return pltpu.CompilerParams(dimension_semantics=sem, vmem_limit_bytes=vmem)


def _layer_norm(x, g, b):
    mu = jnp.mean(x, axis=-1, keepdims=True)
    xc = x - mu
    var = jnp.mean(xc * xc, axis=-1, keepdims=True)
    return xc * lax.rsqrt(var + LN_EPS) * g + b


def _resident(shape, index_map):
    return pl.BlockSpec(shape, index_map, pipeline_mode=pl.Buffered(1))


def _pack_halves(y):
    half = y.shape[1] // 2
    bits = pltpu.bitcast(y.astype(BF16).astype(F32), I32)
    return lax.shift_right_logical(bits[:, :half], 16) | bits[:, half:]


def _unpack_halves(w):
    return pltpu.bitcast(w << 16, F32), pltpu.bitcast(w & HIGH_HALF, F32)


def _store_token_tiles(ref, packed, lead=()):
    m = packed.shape[0]
    for c in range(TOK_ROWS):
        ref[lead + (pl.ds(c, m, stride=TOK_ROWS), slice(None))] = packed[:, c * LANES:(c + 1) * LANES]


def _load_token_tiles(ref, m, c, lead=()):
    return ref[lead + (pl.ds(c, m, stride=TOK_ROWS), slice(None))]


def _ln_in_kernel(xp_ref, xs_ref, g_ref, b_ref, h32_ref, h16_ref, *, n_prompt_tiles):
    i = pl.program_id(0)
    x = jnp.where(i < n_prompt_tiles, xp_ref[...], xs_ref[...])
    y = _layer_norm(x, g_ref[...], b_ref[...])
    h32_ref[...] = y
    h16_ref[...] = y.astype(BF16)


def _ln_in(xp, xs, g, b):
    n_p, d = xp.shape
    n_s = xs.shape[0]
    tp, ts = n_p // ROW_TILE, n_s // ROW_TILE
    n = n_p + n_s
    return pl.pallas_call(
        functools.partial(_ln_in_kernel, n_prompt_tiles=tp),
        grid=(tp + ts,),
        in_specs=[
            pl.BlockSpec((ROW_TILE, d), lambda i: (jnp.minimum(i, tp - 1), 0)),
            pl.BlockSpec((ROW_TILE, d), lambda i: (jnp.maximum(i - tp, 0), 0)),
            pl.BlockSpec((1, d), lambda i: (0, 0)),
            pl.BlockSpec((1, d), lambda i: (0, 0)),
        ],
        out_specs=[pl.BlockSpec((ROW_TILE, d), lambda i: (i, 0)),
                   pl.BlockSpec((ROW_TILE, d), lambda i: (i, 0))],
        out_shape=[jax.ShapeDtypeStruct((n, d), F32), jax.ShapeDtypeStruct((n, d), BF16)],
        compiler_params=_params(("arbitrary",)),
        name="ln_in",
    )(xp, xs, g.reshape(1, d), b.reshape(1, d))


def _proj_kernel(x_ref, w_ref, b_ref, o_ref):
    o_ref[...] = (jnp.dot(x_ref[...], w_ref[...], preferred_element_type=F32) + b_ref[...]).astype(o_ref.dtype)


def _pick_tile(n, candidates):
    for c in candidates:
        if n % c == 0:
            return c
    raise ValueError(f"no tile in {candidates} divides {n}")


def _proj(h16, w16, b, l):
    n, d = h16.shape
    d_in = w16.shape[2]
    bm = _pick_tile(n, (1280, 1024, 512, 256))
    bn = 1024
    return pl.pallas_call(
        _proj_kernel,
        grid=(n // bm, d_in // bn),
        in_specs=[
            pl.BlockSpec((bm, d), lambda i, j: (i, 0)),
            pl.BlockSpec((None, d, bn), lambda i, j: (l, 0, j)),
            pl.BlockSpec((None, 1, bn), lambda i, j: (l, 0, j)),
        ],
        out_specs=pl.BlockSpec((bm, bn), lambda i, j: (i, j)),
        out_shape=jax.ShapeDtypeStruct((n, d_in), BF16),
        compiler_params=_params(("arbitrary", "arbitrary")),
        name="proj",
    )(h16, w16, b)


def _mixer_kernel(gate_ref, p_ref, xc_ref, bg_ref, cg_ref, u_ref, v_ref, hp_ref, hxc_ref, hcg_ref,
                  sp_ref, sc_ref, poolw_ref, pscale_ref, convw_ref, lng_ref, lnb_ref, ws_ref, bst_ref,
                  wbr_ref, *rest, tt, pos0, want_v):
    if want_v:
        merged_ref, ptail_ref, ctail_ref, vn_ref, extp_ref, extz_ref, vpad_ref = rest
    else:
        merged_ref, ptail_ref, ctail_ref, extp_ref, extz_ref, vpad_ref = rest
        vn_ref = None
    i = pl.program_id(1)
    first = i == 0
    d_br = p_ref.shape[1]
    d_model = merged_ref.shape[1]
    gdim = d_br // len(POOL_WINDOWS)

    p = p_ref[...].astype(F32)
    extp_ref[0:POOL_HALO, :] = jnp.where(first, sp_ref[0], hp_ref[...].astype(F32))
    extp_ref[POOL_HALO:POOL_HALO + tt, :] = p
    pos = lax.broadcasted_iota(I32, (tt, 1), 0) + (i * tt + pos0)
    a_parts = []
    for g, win in enumerate(POOL_WINDOWS):
        cols = slice(g * gdim, (g + 1) * gdim)
        s = p[:, cols]
        for j in range(1, win):
            s = s + extp_ref[POOL_HALO - j:POOL_HALO - j + tt, cols]
        cnt = jnp.minimum(pos + 1, win).astype(F32)
        mixed = s / cnt - p[:, cols]
        a_parts.append(jnp.dot(mixed.astype(BF16), poolw_ref[g], preferred_element_type=F32))
    a_out = jnp.concatenate(a_parts, axis=-1) * pscale_ref[...]

    z = cg_ref[...].astype(F32) * xc_ref[...].astype(F32)
    extz_ref[0:POOL_HALO, :] = jnp.where(first, sc_ref[0], hcg_ref[...].astype(F32) * hxc_ref[...].astype(F32))
    extz_ref[POOL_HALO:POOL_HALO + tt, :] = z
    y = z * convw_ref[CONV_WIDTH - 1:CONV_WIDTH, :]
    for k in range(CONV_WIDTH - 1):
        off = POOL_HALO - (CONV_WIDTH - 1) + k
        y = y + extz_ref[off:off + tt, :] * convw_ref[k:k + 1, :]
    c_out = bg_ref[...].astype(F32) * y

    vn = _layer_norm(v_ref[...].astype(F32), lng_ref[...], lnb_ref[...])
    if want_v:
        vn_ref[...] = vn
    hd = d_br // GM_HEADS
    tri = (lax.broadcasted_iota(I32, (GM_CHUNK, GM_CHUNK), 0)
           >= lax.broadcasted_iota(I32, (GM_CHUNK, GM_CHUNK), 1))
    wms = [jnp.where(tri, ws_ref[h], jnp.zeros((), BF16)) for h in range(GM_HEADS)]
    s_chunks = []
    for c in range(max(tt // GM_CHUNK, 1)):
        if tt < GM_CHUNK:
            vpad_ref[...] = jnp.zeros(vpad_ref.shape, BF16)
            vpad_ref[0:tt, :] = vn.astype(BF16)
            vchunk = vpad_ref[...]
        else:
            vchunk = vn[c * GM_CHUNK:(c + 1) * GM_CHUNK, :].astype(BF16)
        heads = []
        for h in range(GM_HEADS):
            sh = jnp.dot(wms[h], vchunk[:, h * hd:(h + 1) * hd], preferred_element_type=F32)
            heads.append(sh + bst_ref[:, h:h + 1])
        s_chunks.append(jnp.concatenate(heads, axis=-1)[:min(tt, GM_CHUNK), :])
    s_all = s_chunks[0] if len(s_chunks) == 1 else jnp.concatenate(s_chunks, axis=0)
    g_out = u_ref[...].astype(F32) * s_all

    merged = None
    for n, br in enumerate((a_out, c_out, g_out)):
        t = jnp.dot(br.astype(BF16), wbr_ref[n], preferred_element_type=F32)
        t = jax.nn.sigmoid(gate_ref[:, n * d_model:(n + 1) * d_model].astype(F32)) * t
        merged = t if merged is None else merged + t
    merged_ref[...] = merged.astype(merged_ref.dtype)
    ptail_ref[0] = p[tt - POOL_HALO:tt, :]
    ctail_ref[0] = z[tt - CONV_TAIL:tt, :]


def _mixer(proj, row_off, nb, t, tt, pos0, sp16, sc16, lw, l, want_v):
    d_br = sp16.shape[2]
    d_model = lw["w_branch"].shape[3]
    gate_w = N_BRANCH * d_model
    col0 = gate_w // d_br
    nt = t // tt
    base = row_off // tt

    def row(b, i):
        return base + b * nt + i

    def halo(b, i):
        return jnp.maximum((row_off + b * t + i * tt) // POOL_HALO - 1, 0)

    in_specs = [pl.BlockSpec((tt, gate_w), lambda b, i: (row(b, i), 0))]
    in_specs += [pl.BlockSpec((tt, d_br), functools.partial(lambda b, i, c: (row(b, i), col0 + c), c=c))
                 for c in range(6)]
    in_specs += [pl.BlockSpec((POOL_HALO, d_br), functools.partial(lambda b, i, c: (halo(b, i), col0 + c), c=c))
                 for c in (0, 1, 3)]
    in_specs += [pl.BlockSpec((1, POOL_HALO, d_br), lambda b, i: (b, 0, 0)),
                 pl.BlockSpec((1, POOL_HALO, d_br), lambda b, i: (b, 0, 0))]
    gdim = d_br // len(POOL_WINDOWS)
    in_specs += [
        _resident((None, len(POOL_WINDOWS), gdim, gdim), lambda b, i: (l, 0, 0, 0)),
        _resident((None, 1, d_br), lambda b, i: (l, 0, 0)),
        _resident((None, CONV_WIDTH, d_br), lambda b, i: (l, 0, 0)),
        _resident((None, 1, d_br), lambda b, i: (l, 0, 0)),
        _resident((None, 1, d_br), lambda b, i: (l, 0, 0)),
        _resident((None, GM_HEADS, GM_CHUNK, GM_CHUNK), lambda b, i: (l, 0, 0, 0)),
        _resident((None, GM_CHUNK, GM_HEADS), lambda b, i: (l, 0, 0)),
        _resident((None, N_BRANCH, d_br, d_model), lambda b, i: (l, 0, 0, 0)),
    ]
    out_specs = [pl.BlockSpec((tt, d_model), lambda b, i: (b * nt + i, 0)),
                 pl.BlockSpec((1, POOL_HALO, d_br), lambda b, i: (b, 0, 0)),
                 pl.BlockSpec((1, CONV_TAIL, d_br), lambda b, i: (b, 0, 0))]
    out_shape = [jax.ShapeDtypeStruct((nb * t, d_model), BF16),
                 jax.ShapeDtypeStruct((nb, POOL_HALO, d_br), F32),
                 jax.ShapeDtypeStruct((nb, CONV_TAIL, d_br), F32)]
    if want_v:
        out_specs.append(pl.BlockSpec((tt, d_br), lambda b, i: (b * nt + i, 0)))
        out_shape.append(jax.ShapeDtypeStruct((nb * t, d_br), F32))
    return pl.pallas_call(
        functools.partial(_mixer_kernel, tt=tt, pos0=pos0, want_v=want_v),
        grid=(nb, nt),
        in_specs=in_specs,
        out_specs=out_specs,
        out_shape=out_shape,
        scratch_shapes=[pltpu.VMEM((POOL_HALO + tt, d_br), F32),
                        pltpu.VMEM((POOL_HALO + tt, d_br), F32),
                        pltpu.VMEM((GM_CHUNK, d_br), BF16)],
        compiler_params=_params(("arbitrary", "arbitrary")),
        name=f"mixer_t{t}",
    )(proj, proj, proj, proj, proj, proj, proj, proj, proj, proj, sp16, sc16,
      lw["pool_w"], lw["pool_scale"], lw["conv_w"], lw["gm_ln_g"], lw["gm_ln_b"], lw["gm_ws"], lw["gm_bst"],
      lw["w_branch"])


def _outproj_kernel(mp_ref, ms_ref, h_ref, w_ref, g_ref, b_ref, rwh_ref, rwl_ref, rb_ref,
                    h32_ref, hpk_ref, route_ref, gates_ref, counts_ref, carry_ref,
                    *, n_prompt_tiles, alpha, n_experts):
    i = pl.program_id(0)
    tm = h_ref.shape[0]

    @pl.when(i == 0)
    def _():
        carry_ref[...] = jnp.zeros(carry_ref.shape, F32)

    m = jnp.where(i < n_prompt_tiles, mp_ref[...], ms_ref[...])
    y = alpha * h_ref[...] + jnp.dot(m, w_ref[...], preferred_element_type=F32)
    h1 = _layer_norm(y, g_ref[...], b_ref[...])
    h32_ref[...] = h1
    _store_token_tiles(hpk_ref, _pack_halves(h1))

    hh = h1.astype(BF16)
    hl = (h1 - hh.astype(F32)).astype(BF16)
    logits = (jnp.dot(hh, rwh_ref[...], preferred_element_type=F32)
              + jnp.dot(hl, rwh_ref[...], preferred_element_type=F32)
              + jnp.dot(hh, rwl_ref[...], preferred_element_type=F32)) + rb_ref[...]
    lane = lax.broadcasted_iota(I32, (tm, LANES), 1)
    lane_f = lane.astype(F32)
    neg = jnp.float32(-jnp.inf)
    cur = jnp.where(lane < n_experts, logits, neg)
    vals, idxs = [], []
    for _ in range(TOP_K):
        mx = jnp.max(cur, axis=-1, keepdims=True)
        ik = jnp.min(jnp.where(cur == mx, lane_f, float(LANES)), axis=-1, keepdims=True).astype(I32)
        vals.append(mx)
        idxs.append(ik)
        cur = jnp.where(lane == ik, neg, cur)
    exps = [jnp.exp(v - vals[0]) for v in vals]
    denom = exps[0]
    for e in exps[1:]:
        denom = denom + e

    onehot = jnp.zeros((tm, LANES), F32)
    for ik in idxs:
        onehot = onehot + (lane == ik).astype(F32)
    below = (lax.broadcasted_iota(I32, (tm, tm), 0) > lax.broadcasted_iota(I32, (tm, tm), 1)).astype(BF16)
    prefix = jnp.dot(below, onehot.astype(BF16), preferred_element_type=F32) + carry_ref[...]
    route = jnp.zeros((tm, LANES), I32)
    gates = jnp.zeros((tm, LANES), F32)
    for k in range(TOP_K):
        rank = jnp.sum(jnp.where(lane == idxs[k], prefix, 0.0), axis=-1, keepdims=True).astype(I32)
        route = jnp.where(lane == k, idxs[k], route)
        route = jnp.where(lane == TOP_K + k, rank, route)
        gates = jnp.where(lane == k, exps[k] / denom, gates)
    route_ref[...] = route
    gates_ref[...] = gates
    carry_ref[...] = carry_ref[...] + jnp.sum(onehot, axis=0, keepdims=True)
    counts_ref[...] = jnp.broadcast_to(carry_ref[...], counts_ref.shape)


def _outproj(mp, ms, h32, lw, l, alpha, n_experts):
    n, d = h32.shape
    tp, ts = mp.shape[0] // ROW_TILE, ms.shape[0] // ROW_TILE
    return pl.pallas_call(
        functools.partial(_outproj_kernel, n_prompt_tiles=tp, alpha=alpha, n_experts=n_experts),
        grid=(tp + ts,),
        in_specs=[
            pl.BlockSpec((ROW_TILE, d), lambda i: (jnp.minimum(i, tp - 1), 0)),
            pl.BlockSpec((ROW_TILE, d), lambda i: (jnp.maximum(i - tp, 0), 0)),
            pl.BlockSpec((ROW_TILE, d), lambda i: (i, 0)),
            _resident((None, d, d), lambda i: (l, 0, 0)),
            _resident((None, 1, d), lambda i: (l, 0, 0)),
            _resident((None, 1, d), lambda i: (l, 0, 0)),
            _resident((None, d, LANES), lambda i: (l, 0, 0)),
            _resident((None, d, LANES), lambda i: (l, 0, 0)),
            _resident((None, 1, LANES), lambda i: (l, 0, 0)),
        ],
        out_specs=[pl.BlockSpec((ROW_TILE, d), lambda i: (i, 0)),
                   pl.BlockSpec((ROW_TILE * TOK_ROWS, LANES), lambda i: (i, 0)),
                   pl.BlockSpec((ROW_TILE, LANES), lambda i: (i, 0)),
                   pl.BlockSpec((ROW_TILE, LANES), lambda i: (i, 0)),
                   pl.BlockSpec((8, LANES), lambda i: (0, 0))],
        out_shape=[jax.ShapeDtypeStruct((n, d), F32),
                   jax.ShapeDtypeStruct((n * TOK_ROWS, LANES), I32),
                   jax.ShapeDtypeStruct((n, LANES), I32),
                   jax.ShapeDtypeStruct((n, LANES), F32),
                   jax.ShapeDtypeStruct((8, LANES), F32)],
        scratch_shapes=[pltpu.VMEM((1, LANES), F32)],
        compiler_params=_params(("arbitrary",)),
        name="outproj_router",
    )(mp, ms, h32, lw["w_out"], lw["ln1_g"], lw["ln1_b"], lw["router_wh"], lw["router_wl"], lw["router_b"])


def _token_tile(ref, row):
    return ref.at[pl.ds(pl.multiple_of(row * TOK_ROWS, TOK_ROWS), TOK_ROWS)]


def _dispatch_kernel(zlo_ref, zhi_ref, dest_ref, h_ref, xs_ref, zero_ref, sem, zsem):
    tm = h_ref.shape[0] // TOK_ROWS

    @pl.when(pl.program_id(0) == 0)
    def _():
        zero_ref[...] = jnp.zeros(zero_ref.shape, I32)

        def bulk_copy(row):
            dst = xs_ref.at[pl.ds(pl.multiple_of(row * TOK_ROWS, TOK_ROWS), ZERO_CHUNK * TOK_ROWS)]
            return pltpu.make_async_copy(zero_ref, dst, zsem)

        def tile_copy(row):
            return pltpu.make_async_copy(zero_ref.at[pl.ds(0, TOK_ROWS)], _token_tile(xs_ref, row), zsem)

        def each_range(g, c):
            lo, hi = zlo_ref[g], zhi_ref[g]
            n_bulk = (hi - lo) // ZERO_CHUNK
            rest = lo + n_bulk * ZERO_CHUNK
            lax.fori_loop(0, n_bulk, lambda j, cc: (bulk_copy(lo + j * ZERO_CHUNK).start(), cc)[1], 0)
            lax.fori_loop(rest, hi, lambda row, cc: (tile_copy(row).start(), cc)[1], 0)
            lax.fori_loop(0, n_bulk, lambda j, cc: (bulk_copy(lo + j * ZERO_CHUNK).wait(), cc)[1], 0)
            lax.fori_loop(rest, hi, lambda row, cc: (tile_copy(row).wait(), cc)[1], 0)
            return c

        lax.fori_loop(0, zlo_ref.shape[0], each_range, 0)

    def row_copy(r, k):
        return pltpu.make_async_copy(_token_tile(h_ref, r), _token_tile(xs_ref, dest_ref[r * TOP_K + k]), sem)

    def start(r, c):
        for k in range(TOP_K):
            row_copy(r, k).start(priority=k % 2)
        return c

    def wait(r, c):
        for k in range(TOP_K):
            row_copy(r, k).wait()
        return c

    lax.fori_loop(0, tm, start, 0, unroll=4)
    lax.fori_loop(0, tm, wait, 0, unroll=4)


def _dispatch(hpk, dest_flat, zero_lo, zero_hi, n_rows):
    n = hpk.shape[0] // TOK_ROWS
    grid_spec = pltpu.PrefetchScalarGridSpec(
        num_scalar_prefetch=2,
        grid=(n // ROW_TILE,),
        in_specs=[
            pl.BlockSpec((ROW_TILE * TOP_K,), lambda i, zl, zh: (i,), memory_space=pltpu.SMEM),
            pl.BlockSpec((ROW_TILE * TOK_ROWS, LANES), lambda i, zl, zh: (i, 0)),
        ],
        out_specs=pl.BlockSpec(memory_space=pl.ANY),
        scratch_shapes=[pltpu.VMEM((ZERO_CHUNK * TOK_ROWS, LANES), I32), pltpu.SemaphoreType.DMA(()),
                        pltpu.SemaphoreType.DMA(())],
    )
    return pl.pallas_call(
        _dispatch_kernel,
        grid_spec=grid_spec,
        out_shape=jax.ShapeDtypeStruct((n_rows * TOK_ROWS, LANES), I32),
        compiler_params=_params(("arbitrary",)),
        name="dispatch",
    )(zero_lo, zero_hi, dest_flat, hpk)


def _experts_kernel(be_ref, br_ref, nv_ref, x_ref, wg_ref, wl_ref, bg_ref, bl_ref, wd_ref, bd_ref, o_ref,
                    x16_ref, acc_ref):
    del be_ref, br_ref
    b = pl.program_id(0)
    f = pl.program_id(1)
    nf = pl.num_programs(1)
    n_valid = nv_ref[b]
    tm, d = x16_ref.shape
    half = d // 2

    @pl.when(n_valid > 0)
    def _():
        @pl.when(f == 0)
        def _():
            for c in range(TOK_ROWS):
                lo, hi = _unpack_halves(_load_token_tiles(x_ref, tm, c))
                x16_ref[:, c * LANES:(c + 1) * LANES] = lo.astype(BF16)
                x16_ref[:, half + c * LANES:half + (c + 1) * LANES] = hi.astype(BF16)
            acc_ref[...] = jnp.broadcast_to(bd_ref[...], acc_ref.shape)

        def run(groups):
            wg16 = wg_ref[...].astype(BF16)
            wl16 = wl_ref[...].astype(BF16)
            wd16 = wd_ref[...].astype(BF16)
            row0 = 0
            for cm in groups:
                rows = slice(row0, row0 + cm)
                row0 += cm
                x = x16_ref[rows, :]
                hg = jnp.dot(x, wg16, preferred_element_type=F32) + bg_ref[...]
                hl = jnp.dot(x, wl16, preferred_element_type=F32) + bl_ref[...]
                glu = jnp.minimum(hg, SWIGLU_LIMIT)
                lin = jnp.clip(hl, -SWIGLU_LIMIT, SWIGLU_LIMIT)
                a = glu * jax.nn.sigmoid(SWIGLU_ALPHA * glu) * (lin + 1.0)
                acc_ref[rows, :] += jnp.dot(a.astype(BF16), wd16, preferred_element_type=F32)

        n_sub = (n_valid + (MOE_SUB - 1)) // MOE_SUB
        for q in range(1, tm // MOE_SUB + 1):
            groups = ((q - 1) * MOE_SUB, MOE_SUB) if q > 1 else (MOE_SUB,)
            pl.when(n_sub == q)(functools.partial(run, groups))

        @pl.when(f == nf - 1)
        def _():
            _store_token_tiles(o_ref, _pack_halves(acc_ref[...]))

    @pl.when(jnp.logical_and(n_valid == 0, f == 0))
    def _():
        o_ref[...] = jnp.zeros(o_ref.shape, I32)


def _experts(xs, block_e, block_row, block_valid, lw, l):
    rows = xs.shape[0] // TOK_ROWS
    d = lw["exp_w_down"].shape[3]
    d_ff = lw["exp_w_up"].shape[3] // 2
    nf = d_ff // MOE_FF_TILE
    n_blocks = rows // MOE_BLOCK
    tf = MOE_FF_TILE
    grid_spec = pltpu.PrefetchScalarGridSpec(
        num_scalar_prefetch=3,
        grid=(n_blocks, nf),
        in_specs=[
            pl.BlockSpec((MOE_BLOCK * TOK_ROWS, LANES), lambda b, f, be, br, nv: (br[b], 0)),
            pl.BlockSpec((None, None, d, tf), lambda b, f, be, br, nv: (l, be[b], 0, f)),
            pl.BlockSpec((None, None, d, tf), lambda b, f, be, br, nv: (l, be[b], 0, nf + f)),
            pl.BlockSpec((None, None, 1, tf), lambda b, f, be, br, nv: (l, be[b], 0, f)),
            pl.BlockSpec((None, None, 1, tf), lambda b, f, be, br, nv: (l, be[b], 0, nf + f)),
            pl.BlockSpec((None, None, tf, d), lambda b, f, be, br, nv: (l, be[b], f, 0)),
            pl.BlockSpec((None, None, 1, d), lambda b, f, be, br, nv: (l, be[b], 0, 0)),
        ],
        out_specs=pl.BlockSpec((MOE_BLOCK * TOK_ROWS, LANES), lambda b, f, be, br, nv: (b, 0)),
        scratch_shapes=[pltpu.VMEM((MOE_BLOCK, d), BF16), pltpu.VMEM((MOE_BLOCK, d), F32)],
    )
    return pl.pallas_call(
        _experts_kernel,
        grid_spec=grid_spec,
        out_shape=jax.ShapeDtypeStruct((rows * TOK_ROWS, LANES), I32),
        compiler_params=_params(("arbitrary", "arbitrary")),
        name="experts",
    )(block_e, block_row, block_valid, xs, lw["exp_w_up"], lw["exp_w_up"], lw["exp_b_up"], lw["exp_b_up"],
      lw["exp_w_down"], lw["exp_b_down"])


def _combine_kernel(dest_ref, ys_ref, gates_ref, h_ref, g_ref, b_ref, h32_ref, h16_ref, buf_ref, sem, *, alpha):
    tm = h_ref.shape[0]

    def row_copy(r, k):
        return pltpu.make_async_copy(_token_tile(ys_ref, dest_ref[r * TOP_K + k]), _token_tile(buf_ref.at[k], r), sem)

    def start(r, c):
        for k in range(TOP_K):
            row_copy(r, k).start(priority=k % 2)
        return c

    def wait(r, c):
        for k in range(TOP_K):
            row_copy(r, k).wait()
        return c

    lax.fori_loop(0, tm, start, 0, unroll=4)
    lax.fori_loop(0, tm, wait, 0, unroll=4)
    gates = gates_ref[...]
    los, his = [], []
    for c in range(TOK_ROWS):
        lo_c = hi_c = None
        for k in range(TOP_K):
            lo, hi = _unpack_halves(_load_token_tiles(buf_ref, tm, c, lead=(k,)))
            gk = gates[:, k:k + 1]
            lo_c = lo * gk if lo_c is None else lo_c + lo * gk
            hi_c = hi * gk if hi_c is None else hi_c + hi * gk
        los.append(lo_c)
        his.append(hi_c)
    moe = jnp.concatenate(los + his, axis=-1)
    y = _layer_norm(alpha * h_ref[...] + moe, g_ref[...], b_ref[...])
    h32_ref[...] = y
    h16_ref[...] = y.astype(BF16)


def _combine(ys, dest_flat, gates, h32, g, b, l, alpha):
    n, d = h32.shape
    return pl.pallas_call(
        functools.partial(_combine_kernel, alpha=alpha),
        grid=(n // ROW_TILE,),
        in_specs=[
            pl.BlockSpec((ROW_TILE * TOP_K,), lambda i: (i,), memory_space=pltpu.SMEM),
            pl.BlockSpec(memory_space=pl.ANY),
            pl.BlockSpec((ROW_TILE, LANES), lambda i: (i, 0)),
            pl.BlockSpec((ROW_TILE, d), lambda i: (i, 0)),
            pl.BlockSpec((None, 1, d), lambda i: (l, 0, 0)),
            pl.BlockSpec((None, 1, d), lambda i: (l, 0, 0)),
        ],
        out_specs=[pl.BlockSpec((ROW_TILE, d), lambda i: (i, 0)),
                   pl.BlockSpec((ROW_TILE, d), lambda i: (i, 0))],
        out_shape=[jax.ShapeDtypeStruct((n, d), F32), jax.ShapeDtypeStruct((n, d), BF16)],
        scratch_shapes=[pltpu.VMEM((TOP_K, ROW_TILE * TOK_ROWS, LANES), I32), pltpu.SemaphoreType.DMA(())],
        compiler_params=_params(("arbitrary",)),
        name="combine",
    )(dest_flat, ys, gates, h32, g, b)


def _routing_tables(route, counts_f, n_experts, n_blocks):
    idx = route[:, :TOP_K]
    rank = route[:, TOP_K:2 * TOP_K]
    counts = counts_f[0, :n_experts].astype(I32)
    padded = (counts + MOE_BLOCK - 1) // MOE_BLOCK * MOE_BLOCK
    pad_end = jnp.cumsum(padded)
    pad_start = pad_end - padded
    dest = (pad_start[idx] + rank).reshape(-1).astype(I32)
    n_used = pad_end[-1] // MOE_BLOCK
    blk = jnp.minimum(jnp.arange(n_blocks, dtype=I32), n_used - 1)
    block_e = jnp.sum((pad_end[None, :] <= (blk * MOE_BLOCK)[:, None]).astype(I32), axis=1)
    block_e = jnp.minimum(block_e, n_experts - 1)
    in_use = jnp.arange(n_blocks, dtype=I32) < n_used
    valid = jnp.clip(counts[block_e] - (blk * MOE_BLOCK - pad_start[block_e]), 0, MOE_BLOCK)
    valid = jnp.where(in_use, valid, 0).astype(I32)
    zero_lo = jnp.concatenate([pad_start + counts, pad_end[-1:]]).astype(I32)
    zero_hi = jnp.concatenate([pad_end, jnp.full((1,), n_blocks * MOE_BLOCK, I32)]).astype(I32)
    return dest, block_e.astype(I32), blk.astype(I32), valid, zero_lo, zero_hi


def _pad_state(state, rows):
    return jnp.pad(state, ((0, 0), (0, 0), (rows - state.shape[2], 0), (0, 0)))


def kernel(x_prompt, x_sample, state_pool, state_conv, ln_in_g, ln_in_b, w_in, b_in, pool_w, pool_scale, conv_w,
           gm_ln_g, gm_ln_b, gm_ws, gm_bs, w_branch, w_out, ln1_g, ln1_b, router_w, router_b, exp_w_up,
           exp_b_up, exp_w_down, exp_b_down, ln2_g, ln2_b):
    bp, seq, d = x_prompt.shape
    bs, dseq, _ = x_sample.shape
    depth = w_in.shape[0]
    d_br = pool_scale.shape[1]
    n_experts = router_w.shape[2]
    alpha = float((2 * depth) ** 0.25)
    n_p, n_s = bp * seq, bs * dseq
    n = n_p + n_s
    n_blocks = -(-(n * TOP_K) // MOE_BLOCK) + n_experts
    pool_state = POOL_WINDOWS[-1] - 1

    rw = jnp.pad(router_w, ((0, 0), (0, 0), (0, LANES - n_experts)))
    rwh = rw.astype(BF16)
    lw = dict(
        pool_w=pool_w.astype(BF16), pool_scale=pool_scale.reshape(depth, 1, d_br), conv_w=conv_w,
        gm_ln_g=gm_ln_g.reshape(depth, 1, d_br), gm_ln_b=gm_ln_b.reshape(depth, 1, d_br),
        gm_ws=gm_ws.astype(BF16), gm_bst=jnp.swapaxes(gm_bs, 1, 2), w_branch=w_branch.astype(BF16),
        w_out=w_out.astype(BF16), ln1_g=ln1_g.reshape(depth, 1, d), ln1_b=ln1_b.reshape(depth, 1, d),
        router_wh=rwh, router_wl=(rw - rwh.astype(F32)).astype(BF16),
        router_b=jnp.pad(router_b, ((0, 0), (0, LANES - n_experts))).reshape(depth, 1, LANES),
        exp_w_up=exp_w_up, exp_b_up=exp_b_up.reshape(depth, n_experts, 1, -1),
        exp_w_down=exp_w_down, exp_b_down=exp_b_down.reshape(depth, n_experts, 1, d),
    )
    w_in16 = w_in.astype(BF16)
    b_in3 = b_in.reshape(depth, 1, -1)
    ln2_g3, ln2_b3 = ln2_g.reshape(depth, 1, d), ln2_b.reshape(depth, 1, d)
    sp16 = _pad_state(state_pool, POOL_HALO)
    sc16 = _pad_state(state_conv, POOL_HALO)
    zero_state = jnp.zeros((bp, POOL_HALO, d_br), F32)

    h32, h16 = _ln_in(x_prompt.reshape(n_p, d), x_sample.reshape(n_s, d), ln_in_g, ln_in_b)
    new_pool_p, new_conv_p, new_pool_s, new_conv_s, v_rows = [], [], [], [], []
    for l in range(depth):
        proj = _proj(h16, w_in16, b_in3, l)
        mp, ptail_p, ctail_p = _mixer(proj, 0, bp, seq, MIX_TILE, 0, zero_state, zero_state, lw, l, False)
        ms, ptail_s, ctail_s, vn = _mixer(proj, n_p, bs, dseq, dseq, PAST_LEN, sp16[l], sc16[l], lw, l, True)
        h1, hpk, route, gates, counts = _outproj(mp, ms, h32, lw, l, alpha, n_experts)
        dest, block_e, block_row, block_valid, zero_lo, zero_hi = _routing_tables(route, counts, n_experts, n_blocks)
        xs = _dispatch(hpk, dest, zero_lo, zero_hi, n_blocks * MOE_BLOCK)
        ys = _experts(xs, block_e, block_row, block_valid, lw, l)
        h32, h16 = _combine(ys, dest, gates, h1, ln2_g3, ln2_b3, l, alpha)
        new_pool_p.append(ptail_p[:, POOL_HALO - pool_state:, :])
        new_conv_p.append(ctail_p[:, CONV_TAIL - (CONV_WIDTH - 1):, :])
        new_pool_s.append(ptail_s[:, POOL_HALO - pool_state:, :])
        new_conv_s.append(ctail_s[:, CONV_TAIL - (CONV_WIDTH - 1):, :])
        v_rows.append(vn.reshape(bs, dseq, d_br))
    y_prompt = h32[:n_p].reshape(bp, seq, d)
    y_sample = h32[n_p:].reshape(bs, dseq, d)
    return (y_prompt, y_sample, jnp.stack(new_pool_p), jnp.stack(new_conv_p), jnp.stack(new_pool_s),
            jnp.stack(new_conv_s), jnp.stack(v_rows))
```

```python
import functools

import jax
import jax.numpy as jnp
from jax import lax
from jax.experimental import pallas as pl
from jax.experimental.pallas import tpu as pltpu

F32 = jnp.float32
BF16 = jnp.bfloat16
I32 = jnp.int32

POOL_WINDOWS = (2, 4, 8, 16)
POOL_HALO = 16
CONV_WIDTH = 3
CONV_TAIL = 8
GM_CHUNK = 128
GM_HEADS = 8
N_BRANCH = 3
TOP_K = 4
SWIGLU_LIMIT = 7.0
SWIGLU_ALPHA = 1.702
LN_EPS = 1e-5
PAST_LEN = 4096
LANES = 128

ROW_TILE = 256
MIX_TILE = 256
MOE_BLOCK = 1280
MOE_FF_TILE = 256
ZERO_CHUNK = 64
VMEM_LIMIT = 60 * 1024 * 1024
MOE_SUB = 256
TOK_ROWS = 8
HIGH_HALF = -65536


def _params(sem, vmem=VMEM_LIMIT):
    return pltpu.CompilerParams(dimension_semantics=sem, vmem_limit_bytes=vmem)


def _layer_norm(x, g, b):
    mu = jnp.mean(x, axis=-1, keepdims=True)
    xc = x - mu
    var = jnp.mean(xc * xc, axis=-1, keepdims=True)
    return xc * lax.rsqrt(var + LN_EPS) * g + b


def _resident(shape, index_map):
    return pl.BlockSpec(shape, index_map, pipeline_mode=pl.Buffered(1))


def _pack_halves(y):
    half = y.shape[1] // 2
    bits = pltpu.bitcast(y.astype(BF16).astype(F32), I32)
    return lax.shift_right_logical(bits[:, :half], 16) | bits[:, half:]


def _unpack_halves(w):
    return pltpu.bitcast(w << 16, F32), pltpu.bitcast(w & HIGH_HALF, F32)


def _store_token_tiles(ref, packed, lead=()):
    m = packed.shape[0]
    for c in range(TOK_ROWS):
        ref[lead + (pl.ds(c, m, stride=TOK_ROWS), slice(None))] = packed[:, c * LANES:(c + 1) * LANES]


def _load_token_tiles(ref, m, c, lead=()):
    return ref[lead + (pl.ds(c, m, stride=TOK_ROWS), slice(None))]


def _ln_in_kernel(xp_ref, xs_ref, g_ref, b_ref, h32_ref, h16_ref, *, n_prompt_tiles):
    i = pl.program_id(0)
    x = jnp.where(i < n_prompt_tiles, xp_ref[...], xs_ref[...])
    y = _layer_norm(x, g_ref[...], b_ref[...])
    h32_ref[...] = y
    h16_ref[...] = y.astype(BF16)


def _ln_in(xp, xs, g, b):
    n_p, d = xp.shape
    n_s = xs.shape[0]
    tp, ts = n_p // ROW_TILE, n_s // ROW_TILE
    n = n_p + n_s
    return pl.pallas_call(
        functools.partial(_ln_in_kernel, n_prompt_tiles=tp),
        grid=(tp + ts,),
        in_specs=[
            pl.BlockSpec((ROW_TILE, d), lambda i: (jnp.minimum(i, tp - 1), 0)),
            pl.BlockSpec((ROW_TILE, d), lambda i: (jnp.maximum(i - tp, 0), 0)),
            pl.BlockSpec((1, d), lambda i: (0, 0)),
            pl.BlockSpec((1, d), lambda i: (0, 0)),
        ],
        out_specs=[pl.BlockSpec((ROW_TILE, d), lambda i: (i, 0)),
                   pl.BlockSpec((ROW_TILE, d), lambda i: (i, 0))],
        out_shape=[jax.ShapeDtypeStruct((n, d), F32), jax.ShapeDtypeStruct((n, d), BF16)],
        compiler_params=_params(("arbitrary",)),
        name="ln_in",
    )(xp, xs, g.reshape(1, d), b.reshape(1, d))


def _proj_kernel(x_ref, w_ref, b_ref, o_ref):
    o_ref[...] = (jnp.dot(x_ref[...], w_ref[...], preferred_element_type=F32) + b_ref[...]).astype(o_ref.dtype)


def _pick_tile(n, candidates):
    for c in candidates:
        if n % c == 0:
            return c
    raise ValueError(f"no tile in {candidates} divides {n}")


def _proj(h16, w16, b, l):
    n, d = h16.shape
    d_in = w16.shape[2]
    bm = _pick_tile(n, (1280, 1024, 512, 256))
    bn = 1024
    return pl.pallas_call(
        _proj_kernel,
        grid=(n // bm, d_in // bn),
        in_specs=[
            pl.BlockSpec((bm, d), lambda i, j: (i, 0)),
            pl.BlockSpec((None, d, bn), lambda i, j: (l, 0, j)),
            pl.BlockSpec((None, 1, bn), lambda i, j: (l, 0, j)),
        ],
        out_specs=pl.BlockSpec((bm, bn), lambda i, j: (i, j)),
        out_shape=jax.ShapeDtypeStruct((n, d_in), BF16),
        compiler_params=_params(("arbitrary", "arbitrary")),
        name="proj",
    )(h16, w16, b)


def _mixer_kernel(gate_ref, p_ref, xc_ref, bg_ref, cg_ref, u_ref, v_ref, hp_ref, hxc_ref, hcg_ref,
                  sp_ref, sc_ref, poolw_ref, pscale_ref, convw_ref, lng_ref, lnb_ref, ws_ref, bst_ref,
                  wbr_ref, *rest, tt, pos0, want_v):
    if want_v:
        merged_ref, ptail_ref, ctail_ref, vn_ref, extp_ref, extz_ref, vpad_ref = rest
    else:
        merged_ref, ptail_ref, ctail_ref, extp_ref, extz_ref, vpad_ref = rest
        vn_ref = None
    i = pl.program_id(1)
    first = i == 0
    d_br = p_ref.shape[1]
    d_model = merged_ref.shape[1]
    gdim = d_br // len(POOL_WINDOWS)

    p = p_ref[...].astype(F32)
    extp_ref[0:POOL_HALO, :] = jnp.where(first, sp_ref[0], hp_ref[...].astype(F32))
    extp_ref[POOL_HALO:POOL_HALO + tt, :] = p
    pos = lax.broadcasted_iota(I32, (tt, 1), 0) + (i * tt + pos0)
    a_parts = []
    for g, win in enumerate(POOL_WINDOWS):
        cols = slice(g * gdim, (g + 1) * gdim)
        s = p[:, cols]
        for j in range(1, win):
            s = s + extp_ref[POOL_HALO - j:POOL_HALO - j + tt, cols]
        cnt = jnp.minimum(pos + 1, win).astype(F32)
        mixed = s / cnt - p[:, cols]
        a_parts.append(jnp.dot(mixed.astype(BF16), poolw_ref[g], preferred_element_type=F32))
    a_out = jnp.concatenate(a_parts, axis=-1) * pscale_ref[...]

    z = cg_ref[...].astype(F32) * xc_ref[...].astype(F32)
    extz_ref[0:POOL_HALO, :] = jnp.where(first, sc_ref[0], hcg_ref[...].astype(F32) * hxc_ref[...].astype(F32))
    extz_ref[POOL_HALO:POOL_HALO + tt, :] = z
    y = z * convw_ref[CONV_WIDTH - 1:CONV_WIDTH, :]
    for k in range(CONV_WIDTH - 1):
        off = POOL_HALO - (CONV_WIDTH - 1) + k
        y = y + extz_ref[off:off + tt, :] * convw_ref[k:k + 1, :]
    c_out = bg_ref[...].astype(F32) * y

    vn = _layer_norm(v_ref[...].astype(F32), lng_ref[...], lnb_ref[...])
    if want_v:
        vn_ref[...] = vn
    hd = d_br // GM_HEADS
    tri = (lax.broadcasted_iota(I32, (GM_CHUNK, GM_CHUNK), 0)
           >= lax.broadcasted_iota(I32, (GM_CHUNK, GM_CHUNK), 1))
    wms = [jnp.where(tri, ws_ref[h], jnp.zeros((), BF16)) for h in range(GM_HEADS)]
    s_chunks = []
    for c in range(max(tt // GM_CHUNK, 1)):
        if tt < GM_CHUNK:
            vpad_ref[...] = jnp.zeros(vpad_ref.shape, BF16)
            vpad_ref[0:tt, :] = vn.astype(BF16)
            vchunk = vpad_ref[...]
        else:
            vchunk = vn[c * GM_CHUNK:(c + 1) * GM_CHUNK, :].astype(BF16)
        heads = []
        for h in range(GM_HEADS):
            sh = jnp.dot(wms[h], vchunk[:, h * hd:(h + 1) * hd], preferred_element_type=F32)
            heads.append(sh + bst_ref[:, h:h + 1])
        s_chunks.append(jnp.concatenate(heads, axis=-1)[:min(tt, GM_CHUNK), :])
    s_all = s_chunks[0] if len(s_chunks) == 1 else jnp.concatenate(s_chunks, axis=0)
    g_out = u_ref[...].astype(F32) * s_all

    merged = None
    for n, br in enumerate((a_out, c_out, g_out)):
        t = jnp.dot(br.astype(BF16), wbr_ref[n], preferred_element_type=F32)
        t = jax.nn.sigmoid(gate_ref[:, n * d_model:(n + 1) * d_model].astype(F32)) * t
        merged = t if merged is None else merged + t
    merged_ref[...] = merged.astype(merged_ref.dtype)
    ptail_ref[0] = p[tt - POOL_HALO:tt, :]
    ctail_ref[0] = z[tt - CONV_TAIL:tt, :]


def _mixer(proj, row_off, nb, t, tt, pos0, sp16, sc16, lw, l, want_v):
    d_br = sp16.shape[2]
    d_model = lw["w_branch"].shape[3]
    gate_w = N_BRANCH * d_model
    col0 = gate_w // d_br
    nt = t // tt
    base = row_off // tt

    def row(b, i):
        return base + b * nt + i

    def halo(b, i):
        return jnp.maximum((row_off + b * t + i * tt) // POOL_HALO - 1, 0)

    in_specs = [pl.BlockSpec((tt, gate_w), lambda b, i: (row(b, i), 0))]
    in_specs += [pl.BlockSpec((tt, d_br), functools.partial(lambda b, i, c: (row(b, i), col0 + c), c=c))
                 for c in range(6)]
    in_specs += [pl.BlockSpec((POOL_HALO, d_br), functools.partial(lambda b, i, c: (halo(b, i), col0 + c), c=c))
                 for c in (0, 1, 3)]
    in_specs += [pl.BlockSpec((1, POOL_HALO, d_br), lambda b, i: (b, 0, 0)),
                 pl.BlockSpec((1, POOL_HALO, d_br), lambda b, i: (b, 0, 0))]
    gdim = d_br // len(POOL_WINDOWS)
    in_specs += [
        _resident((None, len(POOL_WINDOWS), gdim, gdim), lambda b, i: (l, 0, 0, 0)),
        _resident((None, 1, d_br), lambda b, i: (l, 0, 0)),
        _resident((None, CONV_WIDTH, d_br), lambda b, i: (l, 0, 0)),
        _resident((None, 1, d_br), lambda b, i: (l, 0, 0)),
        _resident((None, 1, d_br), lambda b, i: (l, 0, 0)),
        _resident((None, GM_HEADS, GM_CHUNK, GM_CHUNK), lambda b, i: (l, 0, 0, 0)),
        _resident((None, GM_CHUNK, GM_HEADS), lambda b, i: (l, 0, 0)),
        _resident((None, N_BRANCH, d_br, d_model), lambda b, i: (l, 0, 0, 0)),
    ]
    out_specs = [pl.BlockSpec((tt, d_model), lambda b, i: (b * nt + i, 0)),
                 pl.BlockSpec((1, POOL_HALO, d_br), lambda b, i: (b, 0, 0)),
                 pl.BlockSpec((1, CONV_TAIL, d_br), lambda b, i: (b, 0, 0))]
    out_shape = [jax.ShapeDtypeStruct((nb * t, d_model), BF16),
                 jax.ShapeDtypeStruct((nb, POOL_HALO, d_br), F32),
                 jax.ShapeDtypeStruct((nb, CONV_TAIL, d_br), F32)]
    if want_v:
        out_specs.append(pl.BlockSpec((tt, d_br), lambda b, i: (b * nt + i, 0)))
        out_shape.append(jax.ShapeDtypeStruct((nb * t, d_br), F32))
    return pl.pallas_call(
        functools.partial(_mixer_kernel, tt=tt, pos0=pos0, want_v=want_v),
        grid=(nb, nt),
        in_specs=in_specs,
        out_specs=out_specs,
        out_shape=out_shape,
        scratch_shapes=[pltpu.VMEM((POOL_HALO + tt, d_br), F32),
                        pltpu.VMEM((POOL_HALO + tt, d_br), F32),
                        pltpu.VMEM((GM_CHUNK, d_br), BF16)],
        compiler_params=_params(("arbitrary", "arbitrary")),
        name=f"mixer_t{t}",
    )(proj, proj, proj, proj, proj, proj, proj, proj, proj, proj, sp16, sc16,
      lw["pool_w"], lw["pool_scale"], lw["conv_w"], lw["gm_ln_g"], lw["gm_ln_b"], lw["gm_ws"], lw["gm_bst"],
      lw["w_branch"])


def _outproj_kernel(mp_ref, ms_ref, h_ref, w_ref, g_ref, b_ref, rwh_ref, rwl_ref, rb_ref,
                    h32_ref, hpk_ref, route_ref, gates_ref, counts_ref, carry_ref,
                    *, n_prompt_tiles, alpha, n_experts):
    i = pl.program_id(0)
    tm = h_ref.shape[0]

    @pl.when(i == 0)
    def _():
        carry_ref[...] = jnp.zeros(carry_ref.shape, F32)

    m = jnp.where(i < n_prompt_tiles, mp_ref[...], ms_ref[...])
    y = alpha * h_ref[...] + jnp.dot(m, w_ref[...], preferred_element_type=F32)
    h1 = _layer_norm(y, g_ref[...], b_ref[...])
    h32_ref[...] = h1
    _store_token_tiles(hpk_ref, _pack_halves(h1))

    hh = h1.astype(BF16)
    hl = (h1 - hh.astype(F32)).astype(BF16)
    logits = (jnp.dot(hh, rwh_ref[...], preferred_element_type=F32)
              + jnp.dot(hl, rwh_ref[...], preferred_element_type=F32)
              + jnp.dot(hh, rwl_ref[...], preferred_element_type=F32)) + rb_ref[...]
    lane = lax.broadcasted_iota(I32, (tm, LANES), 1)
    lane_f = lane.astype(F32)
    neg = jnp.float32(-jnp.inf)
    cur = jnp.where(lane < n_experts, logits, neg)
    vals, idxs = [], []
    for _ in range(TOP_K):
        mx = jnp.max(cur, axis=-1, keepdims=True)
        ik = jnp.min(jnp.where(cur == mx, lane_f, float(LANES)), axis=-1, keepdims=True).astype(I32)
        vals.append(mx)
        idxs.append(ik)
        cur = jnp.where(lane == ik, neg, cur)
    exps = [jnp.exp(v - vals[0]) for v in vals]
    denom = exps[0]
    for e in exps[1:]:
        denom = denom + e

    onehot = jnp.zeros((tm, LANES), F32)
    for ik in idxs:
        onehot = onehot + (lane == ik).astype(F32)
    below = (lax.broadcasted_iota(I32, (tm, tm), 0) > lax.broadcasted_iota(I32, (tm, tm), 1)).astype(BF16)
    prefix = jnp.dot(below, onehot.astype(BF16), preferred_element_type=F32) + carry_ref[...]
    route = jnp.zeros((tm, LANES), I32)
    gates = jnp.zeros((tm, LANES), F32)
    for k in range(TOP_K):
        rank = jnp.sum(jnp.where(lane == idxs[k], prefix, 0.0), axis=-1, keepdims=True).astype(I32)
        route = jnp.where(lane == k, idxs[k], route)
        route = jnp.where(lane == TOP_K + k, rank, route)
        gates = jnp.where(lane == k, exps[k] / denom, gates)
    route_ref[...] = route
    gates_ref[...] = gates
    carry_ref[...] = carry_ref[...] + jnp.sum(onehot, axis=0, keepdims=True)
    counts_ref[...] = jnp.broadcast_to(carry_ref[...], counts_ref.shape)


def _outproj(mp, ms, h32, lw, l, alpha, n_experts):
    n, d = h32.shape
    tp, ts = mp.shape[0] // ROW_TILE, ms.shape[0] // ROW_TILE
    return pl.pallas_call(
        functools.partial(_outproj_kernel, n_prompt_tiles=tp, alpha=alpha, n_experts=n_experts),
        grid=(tp + ts,),
        in_specs=[
            pl.BlockSpec((ROW_TILE, d), lambda i: (jnp.minimum(i, tp - 1), 0)),
            pl.BlockSpec((ROW_TILE, d), lambda i: (jnp.maximum(i - tp, 0), 0)),
            pl.BlockSpec((ROW_TILE, d), lambda i: (i, 0)),
            _resident((None, d, d), lambda i: (l, 0, 0)),
            _resident((None, 1, d), lambda i: (l, 0, 0)),
            _resident((None, 1, d), lambda i: (l, 0, 0)),
            _resident((None, d, LANES), lambda i: (l, 0, 0)),
            _resident((None, d, LANES), lambda i: (l, 0, 0)),
            _resident((None, 1, LANES), lambda i: (l, 0, 0)),
        ],
        out_specs=[pl.BlockSpec((ROW_TILE, d), lambda i: (i, 0)),
                   pl.BlockSpec((ROW_TILE * TOK_ROWS, LANES), lambda i: (i, 0)),
                   pl.BlockSpec((ROW_TILE, LANES), lambda i: (i, 0)),
                   pl.BlockSpec((ROW_TILE, LANES), lambda i: (i, 0)),
                   pl.BlockSpec((8, LANES), lambda i: (0, 0))],
        out_shape=[jax.ShapeDtypeStruct((n, d), F32),
                   jax.ShapeDtypeStruct((n * TOK_ROWS, LANES), I32),
                   jax.ShapeDtypeStruct((n, LANES), I32),
                   jax.ShapeDtypeStruct((n, LANES), F32),
                   jax.ShapeDtypeStruct((8, LANES), F32)],
        scratch_shapes=[pltpu.VMEM((1, LANES), F32)],
        compiler_params=_params(("arbitrary",)),
        name="outproj_router",
    )(mp, ms, h32, lw["w_out"], lw["ln1_g"], lw["ln1_b"], lw["router_wh"], lw["router_wl"], lw["router_b"])


def _token_tile(ref, row):
    return ref.at[pl.ds(pl.multiple_of(row * TOK_ROWS, TOK_ROWS), TOK_ROWS)]


def _dispatch_kernel(zlo_ref, zhi_ref, dest_ref, h_ref, xs_ref, zero_ref, sem, zsem):
    tm = h_ref.shape[0] // TOK_ROWS

    @pl.when(pl.program_id(0) == 0)
    def _():
        zero_ref[...] = jnp.zeros(zero_ref.shape, I32)

        def bulk_copy(row):
            dst = xs_ref.at[pl.ds(pl.multiple_of(row * TOK_ROWS, TOK_ROWS), ZERO_CHUNK * TOK_ROWS)]
            return pltpu.make_async_copy(zero_ref, dst, zsem)

        def tile_copy(row):
            return pltpu.make_async_copy(zero_ref.at[pl.ds(0, TOK_ROWS)], _token_tile(xs_ref, row), zsem)

        def each_range(g, c):
            lo, hi = zlo_ref[g], zhi_ref[g]
            n_bulk = (hi - lo) // ZERO_CHUNK
            rest = lo + n_bulk * ZERO_CHUNK
            lax.fori_loop(0, n_bulk, lambda j, cc: (bulk_copy(lo + j * ZERO_CHUNK).start(), cc)[1], 0)
            lax.fori_loop(rest, hi, lambda row, cc: (tile_copy(row).start(), cc)[1], 0)
            lax.fori_loop(0, n_bulk, lambda j, cc: (bulk_copy(lo + j * ZERO_CHUNK).wait(), cc)[1], 0)
            lax.fori_loop(rest, hi, lambda row, cc: (tile_copy(row).wait(), cc)[1], 0)
            return c

        lax.fori_loop(0, zlo_ref.shape[0], each_range, 0)

    def row_copy(r, k):
        return pltpu.make_async_copy(_token_tile(h_ref, r), _token_tile(xs_ref, dest_ref[r * TOP_K + k]), sem)

    def start(r, c):
        for k in range(TOP_K):
            row_copy(r, k).start(priority=k % 2)
        return c

    def wait(r, c):
        for k in range(TOP_K):
            row_copy(r, k).wait()
        return c

    lax.fori_loop(0, tm, start, 0, unroll=4)
    lax.fori_loop(0, tm, wait, 0, unroll=4)


def _dispatch(hpk, dest_flat, zero_lo, zero_hi, n_rows):
    n = hpk.shape[0] // TOK_ROWS
    grid_spec = pltpu.PrefetchScalarGridSpec(
        num_scalar_prefetch=2,
        grid=(n // ROW_TILE,),
        in_specs=[
            pl.BlockSpec((ROW_TILE * TOP_K,), lambda i, zl, zh: (i,), memory_space=pltpu.SMEM),
            pl.BlockSpec((ROW_TILE * TOK_ROWS, LANES), lambda i, zl, zh: (i, 0)),
        ],
        out_specs=pl.BlockSpec(memory_space=pl.ANY),
        scratch_shapes=[pltpu.VMEM((ZERO_CHUNK * TOK_ROWS, LANES), I32), pltpu.SemaphoreType.DMA(()),
                        pltpu.SemaphoreType.DMA(())],
    )
    return pl.pallas_call(
        _dispatch_kernel,
        grid_spec=grid_spec,
        out_shape=jax.ShapeDtypeStruct((n_rows * TOK_ROWS, LANES), I32),
        compiler_params=_params(("arbitrary",)),
        name="dispatch",
    )(zero_lo, zero_hi, dest_flat, hpk)


def _experts_kernel(be_ref, br_ref, nv_ref, x_ref, wg_ref, wl_ref, bg_ref, bl_ref, wd_ref, bd_ref, o_ref,
                    x16_ref, acc_ref):
    del be_ref, br_ref
    b = pl.program_id(0)
    f = pl.program_id(1)
    nf = pl.num_programs(1)
    n_valid = nv_ref[b]
    tm, d = x16_ref.shape
    half = d // 2

    @pl.when(n_valid > 0)
    def _():
        @pl.when(f == 0)
        def _():
            for c in range(TOK_ROWS):
                lo, hi = _unpack_halves(_load_token_tiles(x_ref, tm, c))
                x16_ref[:, c * LANES:(c + 1) * LANES] = lo.astype(BF16)
                x16_ref[:, half + c * LANES:half + (c + 1) * LANES] = hi.astype(BF16)
            acc_ref[...] = jnp.broadcast_to(bd_ref[...], acc_ref.shape)

        def run(groups):
            wg16 = wg_ref[...].astype(BF16)
            wl16 = wl_ref[...].astype(BF16)
            wd16 = wd_ref[...].astype(BF16)
            row0 = 0
            for cm in groups:
                rows = slice(row0, row0 + cm)
                row0 += cm
                x = x16_ref[rows, :]
                hg = jnp.dot(x, wg16, preferred_element_type=F32) + bg_ref[...]
                hl = jnp.dot(x, wl16, preferred_element_type=F32) + bl_ref[...]
                glu = jnp.minimum(hg, SWIGLU_LIMIT)
                lin = jnp.clip(hl, -SWIGLU_LIMIT, SWIGLU_LIMIT)
                a = glu * jax.nn.sigmoid(SWIGLU_ALPHA * glu) * (lin + 1.0)
                acc_ref[rows, :] += jnp.dot(a.astype(BF16), wd16, preferred_element_type=F32)

        n_sub = (n_valid + (MOE_SUB - 1)) // MOE_SUB
        for q in range(1, tm // MOE_SUB + 1):
            groups = ((q - 1) * MOE_SUB, MOE_SUB) if q > 1 else (MOE_SUB,)
            pl.when(n_sub == q)(functools.partial(run, groups))

        @pl.when(f == nf - 1)
        def _():
            _store_token_tiles(o_ref, _pack_halves(acc_ref[...]))

    @pl.when(jnp.logical_and(n_valid == 0, f == 0))
    def _():
        o_ref[...] = jnp.zeros(o_ref.shape, I32)


def _experts(xs, block_e, block_row, block_valid, lw, l):
    rows = xs.shape[0] // TOK_ROWS
    d = lw["exp_w_down"].shape[3]
    d_ff = lw["exp_w_up"].shape[3] // 2
    nf = d_ff // MOE_FF_TILE
    n_blocks = rows // MOE_BLOCK
    tf = MOE_FF_TILE
    grid_spec = pltpu.PrefetchScalarGridSpec(
        num_scalar_prefetch=3,
        grid=(n_blocks, nf),
        in_specs=[
            pl.BlockSpec((MOE_BLOCK * TOK_ROWS, LANES), lambda b, f, be, br, nv: (br[b], 0)),
            pl.BlockSpec((None, None, d, tf), lambda b, f, be, br, nv: (l, be[b], 0, f)),
            pl.BlockSpec((None, None, d, tf), lambda b, f, be, br, nv: (l, be[b], 0, nf + f)),
            pl.BlockSpec((None, None, 1, tf), lambda b, f, be, br, nv: (l, be[b], 0, f)),
            pl.BlockSpec((None, None, 1, tf), lambda b, f, be, br, nv: (l, be[b], 0, nf + f)),
            pl.BlockSpec((None, None, tf, d), lambda b, f, be, br, nv: (l, be[b], f, 0)),
            pl.BlockSpec((None, None, 1, d), lambda b, f, be, br, nv: (l, be[b], 0, 0)),
        ],
        out_specs=pl.BlockSpec((MOE_BLOCK * TOK_ROWS, LANES), lambda b, f, be, br, nv: (b, 0)),
        scratch_shapes=[pltpu.VMEM((MOE_BLOCK, d), BF16), pltpu.VMEM((MOE_BLOCK, d), F32)],
    )
    return pl.pallas_call(
        _experts_kernel,
        grid_spec=grid_spec,
        out_shape=jax.ShapeDtypeStruct((rows * TOK_ROWS, LANES), I32),
        compiler_params=_params(("arbitrary", "arbitrary")),
        name="experts",
    )(block_e, block_row, block_valid, xs, lw["exp_w_up"], lw["exp_w_up"], lw["exp_b_up"], lw["exp_b_up"],
      lw["exp_w_down"], lw["exp_b_down"])


def _combine_kernel(dest_ref, dnext_ref, ys_ref, gates_ref, h_ref, g_ref, b_ref, h32_ref, h16_ref, buf_ref, sem,
                    *, alpha):
    i = pl.program_id(0)
    tm = h_ref.shape[0]
    slot = i % 2

    def row_copy(idx_ref, s, r, k):
        return pltpu.make_async_copy(_token_tile(ys_ref, idx_ref[r * TOP_K + k]), _token_tile(buf_ref.at[s, k], r),
                                     sem.at[s])

    def start_tile(idx_ref, s):
        def start(r, c):
            for k in range(TOP_K):
                row_copy(idx_ref, s, r, k).start(priority=k % 2)
            return c
        lax.fori_loop(0, tm, start, 0, unroll=4)

    @pl.when(i == 0)
    def _():
        start_tile(dest_ref, 0)

    @pl.when(i + 1 < pl.num_programs(0))
    def _():
        start_tile(dnext_ref, 1 - slot)

    def wait(r, c):
        for k in range(TOP_K):
            row_copy(dest_ref, slot, r, k).wait()
        return c

    lax.fori_loop(0, tm, wait, 0, unroll=4)
    gates = gates_ref[...]
    los, his = [], []
    for c in range(TOK_ROWS):
        lo_c = hi_c = None
        for k in range(TOP_K):
            lo, hi = _unpack_halves(_load_token_tiles(buf_ref, tm, c, lead=(slot, k)))
            gk = gates[:, k:k + 1]
            lo_c = lo * gk if lo_c is None else lo_c + lo * gk
            hi_c = hi * gk if hi_c is None else hi_c + hi * gk
        los.append(lo_c)
        his.append(hi_c)
    moe = jnp.concatenate(los + his, axis=-1)
    y = _layer_norm(alpha * h_ref[...] + moe, g_ref[...], b_ref[...])
    h32_ref[...] = y
    h16_ref[...] = y.astype(BF16)


def _combine(ys, dest_flat, gates, h32, g, b, l, alpha):
    n, d = h32.shape
    nt = n // ROW_TILE
    return pl.pallas_call(
        functools.partial(_combine_kernel, alpha=alpha),
        grid=(nt,),
        in_specs=[
            pl.BlockSpec((ROW_TILE * TOP_K,), lambda i: (i,), memory_space=pltpu.SMEM),
            pl.BlockSpec((ROW_TILE * TOP_K,), lambda i: (jnp.minimum(i + 1, nt - 1),), memory_space=pltpu.SMEM),
            pl.BlockSpec(memory_space=pl.ANY),
            pl.BlockSpec((ROW_TILE, LANES), lambda i: (i, 0)),
            pl.BlockSpec((ROW_TILE, d), lambda i: (i, 0)),
            pl.BlockSpec((None, 1, d), lambda i: (l, 0, 0)),
            pl.BlockSpec((None, 1, d), lambda i: (l, 0, 0)),
        ],
        out_specs=[pl.BlockSpec((ROW_TILE, d), lambda i: (i, 0)),
                   pl.BlockSpec((ROW_TILE, d), lambda i: (i, 0))],
        out_shape=[jax.ShapeDtypeStruct((n, d), F32), jax.ShapeDtypeStruct((n, d), BF16)],
        scratch_shapes=[pltpu.VMEM((2, TOP_K, ROW_TILE * TOK_ROWS, LANES), I32), pltpu.SemaphoreType.DMA((2,))],
        compiler_params=_params(("arbitrary",)),
        name="combine",
    )(dest_flat, dest_flat, ys, gates, h32, g, b)


def _routing_tables(route, counts_f, n_experts, n_blocks):
    idx = route[:, :TOP_K]
    rank = route[:, TOP_K:2 * TOP_K]
    counts = counts_f[0, :n_experts].astype(I32)
    padded = (counts + MOE_BLOCK - 1) // MOE_BLOCK * MOE_BLOCK
    pad_end = jnp.cumsum(padded)
    pad_start = pad_end - padded
    dest = (pad_start[idx] + rank).reshape(-1).astype(I32)
    n_used = pad_end[-1] // MOE_BLOCK
    blk = jnp.minimum(jnp.arange(n_blocks, dtype=I32), n_used - 1)
    block_e = jnp.sum((pad_end[None, :] <= (blk * MOE_BLOCK)[:, None]).astype(I32), axis=1)
    block_e = jnp.minimum(block_e, n_experts - 1)
    in_use = jnp.arange(n_blocks, dtype=I32) < n_used
    valid = jnp.clip(counts[block_e] - (blk * MOE_BLOCK - pad_start[block_e]), 0, MOE_BLOCK)
    valid = jnp.where(in_use, valid, 0).astype(I32)
    zero_lo = jnp.concatenate([pad_start + counts, pad_end[-1:]]).astype(I32)
    zero_hi = jnp.concatenate([pad_end, jnp.full((1,), n_blocks * MOE_BLOCK, I32)]).astype(I32)
    return dest, block_e.astype(I32), blk.astype(I32), valid, zero_lo, zero_hi


def _pad_state(state, rows):
    return jnp.pad(state, ((0, 0), (0, 0), (rows - state.shape[2], 0), (0, 0)))


def kernel(x_prompt, x_sample, state_pool, state_conv, ln_in_g, ln_in_b, w_in, b_in, pool_w, pool_scale, conv_w,
           gm_ln_g, gm_ln_b, gm_ws, gm_bs, w_branch, w_out, ln1_g, ln1_b, router_w, router_b, exp_w_up,
           exp_b_up, exp_w_down, exp_b_down, ln2_g, ln2_b):
    bp, seq, d = x_prompt.shape
    bs, dseq, _ = x_sample.shape
    depth = w_in.shape[0]
    d_br = pool_scale.shape[1]
    n_experts = router_w.shape[2]
    alpha = float((2 * depth) ** 0.25)
    n_p, n_s = bp * seq, bs * dseq
    n = n_p + n_s
    n_blocks = -(-(n * TOP_K) // MOE_BLOCK) + n_experts
    pool_state = POOL_WINDOWS[-1] - 1

    rw = jnp.pad(router_w, ((0, 0), (0, 0), (0, LANES - n_experts)))
    rwh = rw.astype(BF16)
    lw = dict(
        pool_w=pool_w.astype(BF16), pool_scale=pool_scale.reshape(depth, 1, d_br), conv_w=conv_w,
        gm_ln_g=gm_ln_g.reshape(depth, 1, d_br), gm_ln_b=gm_ln_b.reshape(depth, 1, d_br),
        gm_ws=gm_ws.astype(BF16), gm_bst=jnp.swapaxes(gm_bs, 1, 2), w_branch=w_branch.astype(BF16),
        w_out=w_out.astype(BF16), ln1_g=ln1_g.reshape(depth, 1, d), ln1_b=ln1_b.reshape(depth, 1, d),
        router_wh=rwh, router_wl=(rw - rwh.astype(F32)).astype(BF16),
        router_b=jnp.pad(router_b, ((0, 0), (0, LANES - n_experts))).reshape(depth, 1, LANES),
        exp_w_up=exp_w_up, exp_b_up=exp_b_up.reshape(depth, n_experts, 1, -1),
        exp_w_down=exp_w_down, exp_b_down=exp_b_down.reshape(depth, n_experts, 1, d),
    )
    w_in16 = w_in.astype(BF16)
    b_in3 = b_in.reshape(depth, 1, -1)
    ln2_g3, ln2_b3 = ln2_g.reshape(depth, 1, d), ln2_b.reshape(depth, 1, d)
    sp16 = _pad_state(state_pool, POOL_HALO)
    sc16 = _pad_state(state_conv, POOL_HALO)
    zero_state = jnp.zeros((bp, POOL_HALO, d_br), F32)

    h32, h16 = _ln_in(x_prompt.reshape(n_p, d), x_sample.reshape(n_s, d), ln_in_g, ln_in_b)
    new_pool_p, new_conv_p, new_pool_s, new_conv_s, v_rows = [], [], [], [], []
    for l in range(depth):
        proj = _proj(h16, w_in16, b_in3, l)
        mp, ptail_p, ctail_p = _mixer(proj, 0, bp, seq, MIX_TILE, 0, zero_state, zero_state, lw, l, False)
        ms, ptail_s, ctail_s, vn = _mixer(proj, n_p, bs, dseq, dseq, PAST_LEN, sp16[l], sc16[l], lw, l, True)
        h1, hpk, route, gates, counts = _outproj(mp, ms, h32, lw, l, alpha, n_experts)
        dest, block_e, block_row, block_valid, zero_lo, zero_hi = _routing_tables(route, counts, n_experts, n_blocks)
        xs = _dispatch(hpk, dest, zero_lo, zero_hi, n_blocks * MOE_BLOCK)
        ys = _experts(xs, block_e, block_row, block_valid, lw, l)
        h32, h16 = _combine(ys, dest, gates, h1, ln2_g3, ln2_b3, l, alpha)
        new_pool_p.append(ptail_p[:, POOL_HALO - pool_state:, :])
        new_conv_p.append(ctail_p[:, CONV_TAIL - (CONV_WIDTH - 1):, :])
        new_pool_s.append(ptail_s[:, POOL_HALO - pool_state:, :])
        new_conv_s.append(ctail_s[:, CONV_TAIL - (CONV_WIDTH - 1):, :])
        v_rows.append(vn.reshape(bs, dseq, d_br))
    y_prompt = h32[:n_p].reshape(bp, seq, d)
    y_sample = h32[n_p:].reshape(bs, dseq, d)
    return (y_prompt, y_sample, jnp.stack(new_pool_p), jnp.stack(new_conv_p), jnp.stack(new_pool_s),
            jnp.stack(new_conv_s), jnp.stack(v_rows))
```

```python
import functools

import jax
import jax.numpy as jnp
from jax import lax
from jax.experimental import pallas as pl
from jax.experimental.pallas import tpu as pltpu

F32 = jnp.float32
BF16 = jnp.bfloat16
I32 = jnp.int32

POOL_WINDOWS = (2, 4, 8, 16)
POOL_HALO = 16
CONV_WIDTH = 3
CONV_TAIL = 8
GM_CHUNK = 128
GM_HEADS = 8
N_BRANCH = 3
TOP_K = 4
SWIGLU_LIMIT = 7.0
SWIGLU_ALPHA = 1.702
LN_EPS = 1e-5
PAST_LEN = 4096
LANES = 128

ROW_TILE = 256
MIX_TILE = 256
MOE_BLOCK = 1024
MOE_FF_TILE = 512
ZERO_CHUNK = 64
VMEM_LIMIT = 60 * 1024 * 1024
MOE_SUB = 256
TOK_ROWS = 8
HIGH_HALF = -65536


def _params(sem, vmem=VMEM_LIMIT):
    return pltpu.CompilerParams(dimension_semantics=sem, vmem_limit_bytes=vmem)


def _layer_norm(x, g, b):
    mu = jnp.mean(x, axis=-1, keepdims=True)
    xc = x - mu
    var = jnp.mean(xc * xc, axis=-1, keepdims=True)
    return xc * lax.rsqrt(var + LN_EPS) * g + b


def _resident(shape, index_map):
    return pl.BlockSpec(shape, index_map, pipeline_mode=pl.Buffered(1))


def _pack_halves(y):
    half = y.shape[1] // 2
    bits = pltpu.bitcast(y.astype(BF16).astype(F32), I32)
    return lax.shift_right_logical(bits[:, :half], 16) | bits[:, half:]


def _unpack_halves(w):
    return pltpu.bitcast(w << 16, F32), pltpu.bitcast(w & HIGH_HALF, F32)


def _store_token_tiles(ref, packed, lead=()):
    m = packed.shape[0]
    for c in range(TOK_ROWS):
        ref[lead + (pl.ds(c, m, stride=TOK_ROWS), slice(None))] = packed[:, c * LANES:(c + 1) * LANES]


def _load_token_tiles(ref, m, c, lead=()):
    return ref[lead + (pl.ds(c, m, stride=TOK_ROWS), slice(None))]


def _ln_in_kernel(xp_ref, xs_ref, g_ref, b_ref, h32_ref, h16_ref, *, n_prompt_tiles):
    i = pl.program_id(0)
    x = jnp.where(i < n_prompt_tiles, xp_ref[...], xs_ref[...])
    y = _layer_norm(x, g_ref[...], b_ref[...])
    h32_ref[...] = y
    h16_ref[...] = y.astype(BF16)


def _ln_in(xp, xs, g, b):
    n_p, d = xp.shape
    n_s = xs.shape[0]
    tp, ts = n_p // ROW_TILE, n_s // ROW_TILE
    n = n_p + n_s
    return pl.pallas_call(
        functools.partial(_ln_in_kernel, n_prompt_tiles=tp),
        grid=(tp + ts,),
        in_specs=[
            pl.BlockSpec((ROW_TILE, d), lambda i: (jnp.minimum(i, tp - 1), 0)),
            pl.BlockSpec((ROW_TILE, d), lambda i: (jnp.maximum(i - tp, 0), 0)),
            pl.BlockSpec((1, d), lambda i: (0, 0)),
            pl.BlockSpec((1, d), lambda i: (0, 0)),
        ],
        out_specs=[pl.BlockSpec((ROW_TILE, d), lambda i: (i, 0)),
                   pl.BlockSpec((ROW_TILE, d), lambda i: (i, 0))],
        out_shape=[jax.ShapeDtypeStruct((n, d), F32), jax.ShapeDtypeStruct((n, d), BF16)],
        compiler_params=_params(("arbitrary",)),
        name="ln_in",
    )(xp, xs, g.reshape(1, d), b.reshape(1, d))


def _proj_kernel(x_ref, w_ref, b_ref, o_ref):
    o_ref[...] = (jnp.dot(x_ref[...], w_ref[...], preferred_element_type=F32) + b_ref[...]).astype(o_ref.dtype)


def _pick_tile(n, candidates):
    for c in candidates:
        if n % c == 0:
            return c
    raise ValueError(f"no tile in {candidates} divides {n}")


def _proj(h16, w16, b, l):
    n, d = h16.shape
    d_in = w16.shape[2]
    bm = _pick_tile(n, (1280, 1024, 512, 256))
    bn = 1024
    return pl.pallas_call(
        _proj_kernel,
        grid=(n // bm, d_in // bn),
        in_specs=[
            pl.BlockSpec((bm, d), lambda i, j: (i, 0)),
            pl.BlockSpec((None, d, bn), lambda i, j: (l, 0, j)),
            pl.BlockSpec((None, 1, bn), lambda i, j: (l, 0, j)),
        ],
        out_specs=pl.BlockSpec((bm, bn), lambda i, j: (i, j)),
        out_shape=jax.ShapeDtypeStruct((n, d_in), BF16),
        compiler_params=_params(("arbitrary", "arbitrary")),
        name="proj",
    )(h16, w16, b)


def _mixer_kernel(gate_ref, p_ref, xc_ref, bg_ref, cg_ref, u_ref, v_ref, hp_ref, hxc_ref, hcg_ref,
                  sp_ref, sc_ref, poolw_ref, pscale_ref, convw_ref, lng_ref, lnb_ref, ws_ref, bst_ref,
                  wbr_ref, *rest, tt, pos0, want_v):
    if want_v:
        merged_ref, ptail_ref, ctail_ref, vn_ref, extp_ref, extz_ref, vpad_ref = rest
    else:
        merged_ref, ptail_ref, ctail_ref, extp_ref, extz_ref, vpad_ref = rest
        vn_ref = None
    i = pl.program_id(1)
    first = i == 0
    d_br = p_ref.shape[1]
    d_model = merged_ref.shape[1]
    gdim = d_br // len(POOL_WINDOWS)

    p = p_ref[...].astype(F32)
    extp_ref[0:POOL_HALO, :] = jnp.where(first, sp_ref[0], hp_ref[...].astype(F32))
    extp_ref[POOL_HALO:POOL_HALO + tt, :] = p
    pos = lax.broadcasted_iota(I32, (tt, 1), 0) + (i * tt + pos0)
    a_parts = []
    for g, win in enumerate(POOL_WINDOWS):
        cols = slice(g * gdim, (g + 1) * gdim)
        s = p[:, cols]
        for j in range(1, win):
            s = s + extp_ref[POOL_HALO - j:POOL_HALO - j + tt, cols]
        cnt = jnp.minimum(pos + 1, win).astype(F32)
        mixed = s / cnt - p[:, cols]
        a_parts.append(jnp.dot(mixed.astype(BF16), poolw_ref[g], preferred_element_type=F32))
    a_out = jnp.concatenate(a_parts, axis=-1) * pscale_ref[...]

    z = cg_ref[...].astype(F32) * xc_ref[...].astype(F32)
    extz_ref[0:POOL_HALO, :] = jnp.where(first, sc_ref[0], hcg_ref[...].astype(F32) * hxc_ref[...].astype(F32))
    extz_ref[POOL_HALO:POOL_HALO + tt, :] = z
    y = z * convw_ref[CONV_WIDTH - 1:CONV_WIDTH, :]
    for k in range(CONV_WIDTH - 1):
        off = POOL_HALO - (CONV_WIDTH - 1) + k
        y = y + extz_ref[off:off + tt, :] * convw_ref[k:k + 1, :]
    c_out = bg_ref[...].astype(F32) * y

    vn = _layer_norm(v_ref[...].astype(F32), lng_ref[...], lnb_ref[...])
    if want_v:
        vn_ref[...] = vn
    hd = d_br // GM_HEADS
    tri = (lax.broadcasted_iota(I32, (GM_CHUNK, GM_CHUNK), 0)
           >= lax.broadcasted_iota(I32, (GM_CHUNK, GM_CHUNK), 1))
    wms = [jnp.where(tri, ws_ref[h], jnp.zeros((), BF16)) for h in range(GM_HEADS)]
    s_chunks = []
    for c in range(max(tt // GM_CHUNK, 1)):
        if tt < GM_CHUNK:
            vpad_ref[...] = jnp.zeros(vpad_ref.shape, BF16)
            vpad_ref[0:tt, :] = vn.astype(BF16)
            vchunk = vpad_ref[...]
        else:
            vchunk = vn[c * GM_CHUNK:(c + 1) * GM_CHUNK, :].astype(BF16)
        heads = []
        for h in range(GM_HEADS):
            sh = jnp.dot(wms[h], vchunk[:, h * hd:(h + 1) * hd], preferred_element_type=F32)
            heads.append(sh + bst_ref[:, h:h + 1])
        s_chunks.append(jnp.concatenate(heads, axis=-1)[:min(tt, GM_CHUNK), :])
    s_all = s_chunks[0] if len(s_chunks) == 1 else jnp.concatenate(s_chunks, axis=0)
    g_out = u_ref[...].astype(F32) * s_all

    merged = None
    for n, br in enumerate((a_out, c_out, g_out)):
        t = jnp.dot(br.astype(BF16), wbr_ref[n], preferred_element_type=F32)
        t = jax.nn.sigmoid(gate_ref[:, n * d_model:(n + 1) * d_model].astype(F32)) * t
        merged = t if merged is None else merged + t
    merged_ref[...] = merged.astype(merged_ref.dtype)
    ptail_ref[0] = p[tt - POOL_HALO:tt, :]
    ctail_ref[0] = z[tt - CONV_TAIL:tt, :]


def _mixer(proj, row_off, nb, t, tt, pos0, sp16, sc16, lw, l, want_v):
    d_br = sp16.shape[2]
    d_model = lw["w_branch"].shape[3]
    gate_w = N_BRANCH * d_model
    col0 = gate_w // d_br
    nt = t // tt
    base = row_off // tt

    def row(b, i):
        return base + b * nt + i

    def halo(b, i):
        return jnp.maximum((row_off + b * t + i * tt) // POOL_HALO - 1, 0)

    in_specs = [pl.BlockSpec((tt, gate_w), lambda b, i: (row(b, i), 0))]
    in_specs += [pl.BlockSpec((tt, d_br), functools.partial(lambda b, i, c: (row(b, i), col0 + c), c=c))
                 for c in range(6)]
    in_specs += [pl.BlockSpec((POOL_HALO, d_br), functools.partial(lambda b, i, c: (halo(b, i), col0 + c), c=c))
                 for c in (0, 1, 3)]
    in_specs += [pl.BlockSpec((1, POOL_HALO, d_br), lambda b, i: (b, 0, 0)),
                 pl.BlockSpec((1, POOL_HALO, d_br), lambda b, i: (b, 0, 0))]
    gdim = d_br // len(POOL_WINDOWS)
    in_specs += [
        _resident((None, len(POOL_WINDOWS), gdim, gdim), lambda b, i: (l, 0, 0, 0)),
        _resident((None, 1, d_br), lambda b, i: (l, 0, 0)),
        _resident((None, CONV_WIDTH, d_br), lambda b, i: (l, 0, 0)),
        _resident((None, 1, d_br), lambda b, i: (l, 0, 0)),
        _resident((None, 1, d_br), lambda b, i: (l, 0, 0)),
        _resident((None, GM_HEADS, GM_CHUNK, GM_CHUNK), lambda b, i: (l, 0, 0, 0)),
        _resident((None, GM_CHUNK, GM_HEADS), lambda b, i: (l, 0, 0)),
        _resident((None, N_BRANCH, d_br, d_model), lambda b, i: (l, 0, 0, 0)),
    ]
    out_specs = [pl.BlockSpec((tt, d_model), lambda b, i: (b * nt + i, 0)),
                 pl.BlockSpec((1, POOL_HALO, d_br), lambda b, i: (b, 0, 0)),
                 pl.BlockSpec((1, CONV_TAIL, d_br), lambda b, i: (b, 0, 0))]
    out_shape = [jax.ShapeDtypeStruct((nb * t, d_model), BF16),
                 jax.ShapeDtypeStruct((nb, POOL_HALO, d_br), F32),
                 jax.ShapeDtypeStruct((nb, CONV_TAIL, d_br), F32)]
    if want_v:
        out_specs.append(pl.BlockSpec((tt, d_br), lambda b, i: (b * nt + i, 0)))
        out_shape.append(jax.ShapeDtypeStruct((nb * t, d_br), F32))
    return pl.pallas_call(
        functools.partial(_mixer_kernel, tt=tt, pos0=pos0, want_v=want_v),
        grid=(nb, nt),
        in_specs=in_specs,
        out_specs=out_specs,
        out_shape=out_shape,
        scratch_shapes=[pltpu.VMEM((POOL_HALO + tt, d_br), F32),
                        pltpu.VMEM((POOL_HALO + tt, d_br), F32),
                        pltpu.VMEM((GM_CHUNK, d_br), BF16)],
        compiler_params=_params(("arbitrary", "arbitrary")),
        name=f"mixer_t{t}",
    )(proj, proj, proj, proj, proj, proj, proj, proj, proj, proj, sp16, sc16,
      lw["pool_w"], lw["pool_scale"], lw["conv_w"], lw["gm_ln_g"], lw["gm_ln_b"], lw["gm_ws"], lw["gm_bst"],
      lw["w_branch"])


def _outproj_kernel(mp_ref, ms_ref, h_ref, w_ref, g_ref, b_ref, rwh_ref, rwl_ref, rb_ref,
                    h32_ref, hpk_ref, route_ref, gates_ref, counts_ref, carry_ref,
                    *, n_prompt_tiles, alpha, n_experts):
    i = pl.program_id(0)
    tm = h_ref.shape[0]

    @pl.when(i == 0)
    def _():
        carry_ref[...] = jnp.zeros(carry_ref.shape, F32)

    m = jnp.where(i < n_prompt_tiles, mp_ref[...], ms_ref[...])
    y = alpha * h_ref[...] + jnp.dot(m, w_ref[...], preferred_element_type=F32)
    h1 = _layer_norm(y, g_ref[...], b_ref[...])
    h32_ref[...] = h1
    _store_token_tiles(hpk_ref, _pack_halves(h1))

    hh = h1.astype(BF16)
    hl = (h1 - hh.astype(F32)).astype(BF16)
    logits = (jnp.dot(hh, rwh_ref[...], preferred_element_type=F32)
              + jnp.dot(hl, rwh_ref[...], preferred_element_type=F32)
              + jnp.dot(hh, rwl_ref[...], preferred_element_type=F32)) + rb_ref[...]
    lane = lax.broadcasted_iota(I32, (tm, LANES), 1)
    lane_f = lane.astype(F32)
    neg = jnp.float32(-jnp.inf)
    cur = jnp.where(lane < n_experts, logits, neg)
    vals, idxs = [], []
    for _ in range(TOP_K):
        mx = jnp.max(cur, axis=-1, keepdims=True)
        ik = jnp.min(jnp.where(cur == mx, lane_f, float(LANES)), axis=-1, keepdims=True).astype(I32)
        vals.append(mx)
        idxs.append(ik)
        cur = jnp.where(lane == ik, neg, cur)
    exps = [jnp.exp(v - vals[0]) for v in vals]
    denom = exps[0]
    for e in exps[1:]:
        denom = denom + e

    onehot = jnp.zeros((tm, LANES), F32)
    for ik in idxs:
        onehot = onehot + (lane == ik).astype(F32)
    below = (lax.broadcasted_iota(I32, (tm, tm), 0) > lax.broadcasted_iota(I32, (tm, tm), 1)).astype(BF16)
    prefix = jnp.dot(below, onehot.astype(BF16), preferred_element_type=F32) + carry_ref[...]
    route = jnp.zeros((tm, LANES), I32)
    gates = jnp.zeros((tm, LANES), F32)
    for k in range(TOP_K):
        rank = jnp.sum(jnp.where(lane == idxs[k], prefix, 0.0), axis=-1, keepdims=True).astype(I32)
        route = jnp.where(lane == k, idxs[k], route)
        route = jnp.where(lane == TOP_K + k, rank, route)
        gates = jnp.where(lane == k, exps[k] / denom, gates)
    route_ref[...] = route
    gates_ref[...] = gates
    carry_ref[...] = carry_ref[...] + jnp.sum(onehot, axis=0, keepdims=True)
    counts_ref[...] = jnp.broadcast_to(carry_ref[...], counts_ref.shape)


def _outproj(mp, ms, h32, lw, l, alpha, n_experts):
    n, d = h32.shape
    tp, ts = mp.shape[0] // ROW_TILE, ms.shape[0] // ROW_TILE
    return pl.pallas_call(
        functools.partial(_outproj_kernel, n_prompt_tiles=tp, alpha=alpha, n_experts=n_experts),
        grid=(tp + ts,),
        in_specs=[
            pl.BlockSpec((ROW_TILE, d), lambda i: (jnp.minimum(i, tp - 1), 0)),
            pl.BlockSpec((ROW_TILE, d), lambda i: (jnp.maximum(i - tp, 0), 0)),
            pl.BlockSpec((ROW_TILE, d), lambda i: (i, 0)),
            _resident((None, d, d), lambda i: (l, 0, 0)),
            _resident((None, 1, d), lambda i: (l, 0, 0)),
            _resident((None, 1, d), lambda i: (l, 0, 0)),
            _resident((None, d, LANES), lambda i: (l, 0, 0)),
            _resident((None, d, LANES), lambda i: (l, 0, 0)),
            _resident((None, 1, LANES), lambda i: (l, 0, 0)),
        ],
        out_specs=[pl.BlockSpec((ROW_TILE, d), lambda i: (i, 0)),
                   pl.BlockSpec((ROW_TILE * TOK_ROWS, LANES), lambda i: (i, 0)),
                   pl.BlockSpec((ROW_TILE, LANES), lambda i: (i, 0)),
                   pl.BlockSpec((ROW_TILE, LANES), lambda i: (i, 0)),
                   pl.BlockSpec((8, LANES), lambda i: (0, 0))],
        out_shape=[jax.ShapeDtypeStruct((n, d), F32),
                   jax.ShapeDtypeStruct((n * TOK_ROWS, LANES), I32),
                   jax.ShapeDtypeStruct((n, LANES), I32),
                   jax.ShapeDtypeStruct((n, LANES), F32),
                   jax.ShapeDtypeStruct((8, LANES), F32)],
        scratch_shapes=[pltpu.VMEM((1, LANES), F32)],
        compiler_params=_params(("arbitrary",)),
        name="outproj_router",
    )(mp, ms, h32, lw["w_out"], lw["ln1_g"], lw["ln1_b"], lw["router_wh"], lw["router_wl"], lw["router_b"])


def _token_tile(ref, row):
    return ref.at[pl.ds(pl.multiple_of(row * TOK_ROWS, TOK_ROWS), TOK_ROWS)]


def _dispatch_kernel(zlo_ref, zhi_ref, dest_ref, h_ref, xs_ref, zero_ref, sem, zsem):
    tm = h_ref.shape[0] // TOK_ROWS

    @pl.when(pl.program_id(0) == 0)
    def _():
        zero_ref[...] = jnp.zeros(zero_ref.shape, I32)

        def bulk_copy(row):
            dst = xs_ref.at[pl.ds(pl.multiple_of(row * TOK_ROWS, TOK_ROWS), ZERO_CHUNK * TOK_ROWS)]
            return pltpu.make_async_copy(zero_ref, dst, zsem)

        def tile_copy(row):
            return pltpu.make_async_copy(zero_ref.at[pl.ds(0, TOK_ROWS)], _token_tile(xs_ref, row), zsem)

        def each_range(g, c):
            lo, hi = zlo_ref[g], zhi_ref[g]
            n_bulk = (hi - lo) // ZERO_CHUNK
            rest = lo + n_bulk * ZERO_CHUNK
            lax.fori_loop(0, n_bulk, lambda j, cc: (bulk_copy(lo + j * ZERO_CHUNK).start(), cc)[1], 0)
            lax.fori_loop(rest, hi, lambda row, cc: (tile_copy(row).start(), cc)[1], 0)
            lax.fori_loop(0, n_bulk, lambda j, cc: (bulk_copy(lo + j * ZERO_CHUNK).wait(), cc)[1], 0)
            lax.fori_loop(rest, hi, lambda row, cc: (tile_copy(row).wait(), cc)[1], 0)
            return c

        lax.fori_loop(0, zlo_ref.shape[0], each_range, 0)

    def row_copy(r, k):
        return pltpu.make_async_copy(_token_tile(h_ref, r), _token_tile(xs_ref, dest_ref[r * TOP_K + k]), sem)

    def start(r, c):
        for k in range(TOP_K):
            row_copy(r, k).start(priority=k % 2)
        return c

    def wait(r, c):
        for k in range(TOP_K):
            row_copy(r, k).wait()
        return c

    lax.fori_loop(0, tm, start, 0, unroll=4)
    lax.fori_loop(0, tm, wait, 0, unroll=4)


def _dispatch(hpk, dest_flat, zero_lo, zero_hi, n_rows):
    n = hpk.shape[0] // TOK_ROWS
    grid_spec = pltpu.PrefetchScalarGridSpec(
        num_scalar_prefetch=2,
        grid=(n // ROW_TILE,),
        in_specs=[
            pl.BlockSpec((ROW_TILE * TOP_K,), lambda i, zl, zh: (i,), memory_space=pltpu.SMEM),
            pl.BlockSpec((ROW_TILE * TOK_ROWS, LANES), lambda i, zl, zh: (i, 0)),
        ],
        out_specs=pl.BlockSpec(memory_space=pl.ANY),
        scratch_shapes=[pltpu.VMEM((ZERO_CHUNK * TOK_ROWS, LANES), I32), pltpu.SemaphoreType.DMA(()),
                        pltpu.SemaphoreType.DMA(())],
    )
    return pl.pallas_call(
        _dispatch_kernel,
        grid_spec=grid_spec,
        out_shape=jax.ShapeDtypeStruct((n_rows * TOK_ROWS, LANES), I32),
        compiler_params=_params(("arbitrary",)),
        name="dispatch",
    )(zero_lo, zero_hi, dest_flat, hpk)


def _experts_kernel(be_ref, br_ref, nv_ref, x_ref, wg_ref, wl_ref, bg_ref, bl_ref, wd_ref, bd_ref, o_ref,
                    x16_ref, acc_ref):
    del be_ref, br_ref
    b = pl.program_id(0)
    f = pl.program_id(1)
    nf = pl.num_programs(1)
    n_valid = nv_ref[b]
    tm, d = x16_ref.shape
    half = d // 2

    @pl.when(n_valid > 0)
    def _():
        @pl.when(f == 0)
        def _():
            for c in range(TOK_ROWS):
                lo, hi = _unpack_halves(_load_token_tiles(x_ref, tm, c))
                x16_ref[:, c * LANES:(c + 1) * LANES] = lo.astype(BF16)
                x16_ref[:, half + c * LANES:half + (c + 1) * LANES] = hi.astype(BF16)
            acc_ref[...] = jnp.broadcast_to(bd_ref[...], acc_ref.shape)

        def run(groups):
            wg16 = wg_ref[...].astype(BF16)
            wl16 = wl_ref[...].astype(BF16)
            wd16 = wd_ref[...].astype(BF16)
            row0 = 0
            for cm in groups:
                rows = slice(row0, row0 + cm)
                row0 += cm
                x = x16_ref[rows, :]
                hg = jnp.dot(x, wg16, preferred_element_type=F32) + bg_ref[...]
                hl = jnp.dot(x, wl16, preferred_element_type=F32) + bl_ref[...]
                glu = jnp.minimum(hg, SWIGLU_LIMIT)
                lin = jnp.clip(hl, -SWIGLU_LIMIT, SWIGLU_LIMIT)
                a = glu * jax.nn.sigmoid(SWIGLU_ALPHA * glu) * (lin + 1.0)
                acc_ref[rows, :] += jnp.dot(a.astype(BF16), wd16, preferred_element_type=F32)

        n_sub = (n_valid + (MOE_SUB - 1)) // MOE_SUB
        for q in range(1, tm // MOE_SUB + 1):
            groups = ((q - 1) * MOE_SUB, MOE_SUB) if q > 1 else (MOE_SUB,)
            pl.when(n_sub == q)(functools.partial(run, groups))

        @pl.when(f == nf - 1)
        def _():
            _store_token_tiles(o_ref, _pack_halves(acc_ref[...]))

    @pl.when(jnp.logical_and(n_valid == 0, f == 0))
    def _():
        o_ref[...] = jnp.zeros(o_ref.shape, I32)


def _experts(xs, block_e, block_row, block_valid, lw, l):
    rows = xs.shape[0] // TOK_ROWS
    d = lw["exp_w_down"].shape[3]
    d_ff = lw["exp_w_up"].shape[3] // 2
    nf = d_ff // MOE_FF_TILE
    n_blocks = rows // MOE_BLOCK
    tf = MOE_FF_TILE
    grid_spec = pltpu.PrefetchScalarGridSpec(
        num_scalar_prefetch=3,
        grid=(n_blocks, nf),
        in_specs=[
            pl.BlockSpec((MOE_BLOCK * TOK_ROWS, LANES), lambda b, f, be, br, nv: (br[b], 0)),
            pl.BlockSpec((None, None, d, tf), lambda b, f, be, br, nv: (l, be[b], 0, f)),
            pl.BlockSpec((None, None, d, tf), lambda b, f, be, br, nv: (l, be[b], 0, nf + f)),
            pl.BlockSpec((None, None, 1, tf), lambda b, f, be, br, nv: (l, be[b], 0, f)),
            pl.BlockSpec((None, None, 1, tf), lambda b, f, be, br, nv: (l, be[b], 0, nf + f)),
            pl.BlockSpec((None, None, tf, d), lambda b, f, be, br, nv: (l, be[b], f, 0)),
            pl.BlockSpec((None, None, 1, d), lambda b, f, be, br, nv: (l, be[b], 0, 0)),
        ],
        out_specs=pl.BlockSpec((MOE_BLOCK * TOK_ROWS, LANES), lambda b, f, be, br, nv: (b, 0)),
        scratch_shapes=[pltpu.VMEM((MOE_BLOCK, d), BF16), pltpu.VMEM((MOE_BLOCK, d), F32)],
    )
    return pl.pallas_call(
        _experts_kernel,
        grid_spec=grid_spec,
        out_shape=jax.ShapeDtypeStruct((rows * TOK_ROWS, LANES), I32),
        compiler_params=_params(("arbitrary", "arbitrary")),
        name="experts",
    )(block_e, block_row, block_valid, xs, lw["exp_w_up"], lw["exp_w_up"], lw["exp_b_up"], lw["exp_b_up"],
      lw["exp_w_down"], lw["exp_b_down"])


def _combine_kernel(dest_ref, dnext_ref, ys_ref, gates_ref, h_ref, g_ref, b_ref, h32_ref, h16_ref, buf_ref, sem,
                    *, alpha):
    i = pl.program_id(0)
    tm = h_ref.shape[0]
    slot = i % 2

    def row_copy(idx_ref, s, r, k):
        return pltpu.make_async_copy(_token_tile(ys_ref, idx_ref[r * TOP_K + k]), _token_tile(buf_ref.at[s, k], r),
                                     sem.at[s])

    def start_tile(idx_ref, s):
        def start(r, c):
            for k in range(TOP_K):
                row_copy(idx_ref, s, r, k).start(priority=k % 2)
            return c
        lax.fori_loop(0, tm, start, 0, unroll=4)

    @pl.when(i == 0)
    def _():
        start_tile(dest_ref, 0)

    @pl.when(i + 1 < pl.num_programs(0))
    def _():
        start_tile(dnext_ref, 1 - slot)

    def wait(r, c):
        for k in range(TOP_K):
            row_copy(dest_ref, slot, r, k).wait()
        return c

    lax.fori_loop(0, tm, wait, 0, unroll=4)
    gates = gates_ref[...]
    los, his = [], []
    for c in range(TOK_ROWS):
        lo_c = hi_c = None
        for k in range(TOP_K):
            lo, hi = _unpack_halves(_load_token_tiles(buf_ref, tm, c, lead=(slot, k)))
            gk = gates[:, k:k + 1]
            lo_c = lo * gk if lo_c is None else lo_c + lo * gk
            hi_c = hi * gk if hi_c is None else hi_c + hi * gk
        los.append(lo_c)
        his.append(hi_c)
    moe = jnp.concatenate(los + his, axis=-1)
    y = _layer_norm(alpha * h_ref[...] + moe, g_ref[...], b_ref[...])
    h32_ref[...] = y
    h16_ref[...] = y.astype(BF16)


def _combine(ys, dest_flat, gates, h32, g, b, l, alpha):
    n, d = h32.shape
    nt = n // ROW_TILE
    return pl.pallas_call(
        functools.partial(_combine_kernel, alpha=alpha),
        grid=(nt,),
        in_specs=[
            pl.BlockSpec((ROW_TILE * TOP_K,), lambda i: (i,), memory_space=pltpu.SMEM),
            pl.BlockSpec((ROW_TILE * TOP_K,), lambda i: (jnp.minimum(i + 1, nt - 1),), memory_space=pltpu.SMEM),
            pl.BlockSpec(memory_space=pl.ANY),
            pl.BlockSpec((ROW_TILE, LANES), lambda i: (i, 0)),
            pl.BlockSpec((ROW_TILE, d), lambda i: (i, 0)),
            pl.BlockSpec((None, 1, d), lambda i: (l, 0, 0)),
            pl.BlockSpec((None, 1, d), lambda i: (l, 0, 0)),
        ],
        out_specs=[pl.BlockSpec((ROW_TILE, d), lambda i: (i, 0)),
                   pl.BlockSpec((ROW_TILE, d), lambda i: (i, 0))],
        out_shape=[jax.ShapeDtypeStruct((n, d), F32), jax.ShapeDtypeStruct((n, d), BF16)],
        scratch_shapes=[pltpu.VMEM((2, TOP_K, ROW_TILE * TOK_ROWS, LANES), I32), pltpu.SemaphoreType.DMA((2,))],
        compiler_params=_params(("arbitrary",)),
        name="combine",
    )(dest_flat, dest_flat, ys, gates, h32, g, b)


def _routing_tables(route, counts_f, n_experts, n_blocks):
    idx = route[:, :TOP_K]
    rank = route[:, TOP_K:2 * TOP_K]
    counts = counts_f[0, :n_experts].astype(I32)
    padded = (counts + MOE_BLOCK - 1) // MOE_BLOCK * MOE_BLOCK
    pad_end = jnp.cumsum(padded)
    pad_start = pad_end - padded
    dest = (pad_start[idx] + rank).reshape(-1).astype(I32)
    n_used = pad_end[-1] // MOE_BLOCK
    blk = jnp.minimum(jnp.arange(n_blocks, dtype=I32), n_used - 1)
    block_e = jnp.sum((pad_end[None, :] <= (blk * MOE_BLOCK)[:, None]).astype(I32), axis=1)
    block_e = jnp.minimum(block_e, n_experts - 1)
    in_use = jnp.arange(n_blocks, dtype=I32) < n_used
    valid = jnp.clip(counts[block_e] - (blk * MOE_BLOCK - pad_start[block_e]), 0, MOE_BLOCK)
    valid = jnp.where(in_use, valid, 0).astype(I32)
    zero_lo = jnp.concatenate([pad_start + counts, pad_end[-1:]]).astype(I32)
    zero_hi = jnp.concatenate([pad_end, jnp.full((1,), n_blocks * MOE_BLOCK, I32)]).astype(I32)
    return dest, block_e.astype(I32), blk.astype(I32), valid, zero_lo, zero_hi


def _pad_state(state, rows):
    return jnp.pad(state, ((0, 0), (0, 0), (rows - state.shape[2], 0), (0, 0)))


def kernel(x_prompt, x_sample, state_pool, state_conv, ln_in_g, ln_in_b, w_in, b_in, pool_w, pool_scale, conv_w,
           gm_ln_g, gm_ln_b, gm_ws, gm_bs, w_branch, w_out, ln1_g, ln1_b, router_w, router_b, exp_w_up,
           exp_b_up, exp_w_down, exp_b_down, ln2_g, ln2_b):
    bp, seq, d = x_prompt.shape
    bs, dseq, _ = x_sample.shape
    depth = w_in.shape[0]
    d_br = pool_scale.shape[1]
    n_experts = router_w.shape[2]
    alpha = float((2 * depth) ** 0.25)
    n_p, n_s = bp * seq, bs * dseq
    n = n_p + n_s
    n_blocks = -(-(n * TOP_K) // MOE_BLOCK) + n_experts
    pool_state = POOL_WINDOWS[-1] - 1

    rw = jnp.pad(router_w, ((0, 0), (0, 0), (0, LANES - n_experts)))
    rwh = rw.astype(BF16)
    lw = dict(
        pool_w=pool_w.astype(BF16), pool_scale=pool_scale.reshape(depth, 1, d_br), conv_w=conv_w,
        gm_ln_g=gm_ln_g.reshape(depth, 1, d_br), gm_ln_b=gm_ln_b.reshape(depth, 1, d_br),
        gm_ws=gm_ws.astype(BF16), gm_bst=jnp.swapaxes(gm_bs, 1, 2), w_branch=w_branch.astype(BF16),
        w_out=w_out.astype(BF16), ln1_g=ln1_g.reshape(depth, 1, d), ln1_b=ln1_b.reshape(depth, 1, d),
        router_wh=rwh, router_wl=(rw - rwh.astype(F32)).astype(BF16),
        router_b=jnp.pad(router_b, ((0, 0), (0, LANES - n_experts))).reshape(depth, 1, LANES),
        exp_w_up=exp_w_up, exp_b_up=exp_b_up.reshape(depth, n_experts, 1, -1),
        exp_w_down=exp_w_down, exp_b_down=exp_b_down.reshape(depth, n_experts, 1, d),
    )
    w_in16 = w_in.astype(BF16)
    b_in3 = b_in.reshape(depth, 1, -1)
    ln2_g3, ln2_b3 = ln2_g.reshape(depth, 1, d), ln2_b.reshape(depth, 1, d)
    sp16 = _pad_state(state_pool, POOL_HALO)
    sc16 = _pad_state(state_conv, POOL_HALO)
    zero_state = jnp.zeros((bp, POOL_HALO, d_br), F32)

    h32, h16 = _ln_in(x_prompt.reshape(n_p, d), x_sample.reshape(n_s, d), ln_in_g, ln_in_b)
    new_pool_p, new_conv_p, new_pool_s, new_conv_s, v_rows = [], [], [], [], []
    for l in range(depth):
        proj = _proj(h16, w_in16, b_in3, l)
        mp, ptail_p, ctail_p = _mixer(proj, 0, bp, seq, MIX_TILE, 0, zero_state, zero_state, lw, l, False)
        ms, ptail_s, ctail_s, vn = _mixer(proj, n_p, bs, dseq, dseq, PAST_LEN, sp16[l], sc16[l], lw, l, True)
        h1, hpk, route, gates, counts = _outproj(mp, ms, h32, lw, l, alpha, n_experts)
        dest, block_e, block_row, block_valid, zero_lo, zero_hi = _routing_tables(route, counts, n_experts, n_blocks)
        xs = _dispatch(hpk, dest, zero_lo, zero_hi, n_blocks * MOE_BLOCK)
        ys = _experts(xs, block_e, block_row, block_valid, lw, l)
        h32, h16 = _combine(ys, dest, gates, h1, ln2_g3, ln2_b3, l, alpha)
        new_pool_p.append(ptail_p[:, POOL_HALO - pool_state:, :])
        new_conv_p.append(ctail_p[:, CONV_TAIL - (CONV_WIDTH - 1):, :])
        new_pool_s.append(ptail_s[:, POOL_HALO - pool_state:, :])
        new_conv_s.append(ctail_s[:, CONV_TAIL - (CONV_WIDTH - 1):, :])
        v_rows.append(vn.reshape(bs, dseq, d_br))
    y_prompt = h32[:n_p].reshape(bp, seq, d)
    y_sample = h32[n_p:].reshape(bs, dseq, d)
    return (y_prompt, y_sample, jnp.stack(new_pool_p), jnp.stack(new_conv_p), jnp.stack(new_pool_s),
            jnp.stack(new_conv_s), jnp.stack(v_rows))
```
